```python
import jax
import jax.numpy as jnp
from jax import lax
import numpy as np

D_MODEL = 2048
BATCH = 4
SEQ = 2048
DEPTH = 4
DEC_BATCH = 32
DEC_SEQ = 1
PAST_LEN = 16384
PAGE_SIZE = 128

A_HEADS = 16
A_KV_HEADS = 4
A_HEAD_DIM = D_MODEL // A_HEADS
IDX_HEADS = 16
IDX_DIM = 64
TOPK_MAX = 256
QUERY_BLOCK = 128
B_HEADS = 32
B_KV_HEADS = 4
B_HEAD_DIM = D_MODEL // B_HEADS
WINDOW = 128
FFN_DIM = 7 * D_MODEL // 2
N_EXPERTS = 8
TOP_K_EXPERTS = 2
ROPE_THETA = 500000.0
NORM_EPS = 1e-6
N_A = (DEPTH + 1) // 2
N_B = DEPTH // 2
A_SPLITS = (A_HEADS * A_HEAD_DIM, A_KV_HEADS * A_HEAD_DIM, A_KV_HEADS * A_HEAD_DIM, IDX_HEADS * IDX_DIM, IDX_DIM, IDX_HEADS)
A_PROJ = sum(A_SPLITS)
B_SPLITS = (B_HEADS * B_HEAD_DIM, B_KV_HEADS * B_HEAD_DIM, B_KV_HEADS * B_HEAD_DIM)
B_PROJ = sum(B_SPLITS)

kernel_name = 'dsa_swa_sink_adaln_moe_step'


def _split(p, sizes):
    return jnp.split(p, [int(s) for s in np.cumsum(sizes)[:-1]], axis=-1)


def _rms(x, g):
    xf = x.astype(jnp.float32)
    y = xf * lax.rsqrt(jnp.mean(xf * xf, axis=-1, keepdims=True) + NORM_EPS)
    return (y * g.astype(jnp.float32)).astype(x.dtype)


def _modulate(x, g, shift, scale):
    return _rms(x, g) * (1 + scale[:, None, :]) + shift[:, None, :]


def _adaln(c, w, b):
    return jnp.split(jax.nn.silu(c) @ w + b, 6, axis=-1)


def _rope(x, pos):
    d = x.shape[-1]
    rot = d // 4
    half = rot // 2
    inv = ROPE_THETA ** (-jnp.arange(half, dtype=jnp.float32) * 2.0 / rot)
    ang = pos.astype(jnp.float32)[:, None] * inv[None, :]
    cos = jnp.cos(ang)[:, None, :]
    sin = jnp.sin(ang)[:, None, :]
    xf = x.astype(jnp.float32)
    x1 = xf[..., :half]
    x2 = xf[..., half:rot]
    out = jnp.concatenate([x1 * cos - x2 * sin, x2 * cos + x1 * sin, xf[..., rot:]], axis=-1)
    return out.astype(x.dtype)


def _proj_a(h, w_in, pos):
    b, t, _ = h.shape
    q, k, v, qi, ki, wi = _split(h @ w_in, A_SPLITS)
    q = _rope(q.reshape(b, t, A_HEADS, A_HEAD_DIM), pos)
    k = _rope(k.reshape(b, t, A_KV_HEADS, A_HEAD_DIM), pos)
    v = v.reshape(b, t, A_KV_HEADS, A_HEAD_DIM)
    qi = _rope(qi.reshape(b, t, IDX_HEADS, IDX_DIM), pos)
    ki = _rope(ki.reshape(b, t, 1, IDX_DIM), pos)[:, :, 0]
    wi = wi * (IDX_HEADS ** -0.5 * IDX_DIM ** -0.5)
    return q, k, v, qi, ki, wi


def _index_scores(qi, wi, ki):
    s = jnp.einsum('bqhd,bsd->bqhs', qi, ki).astype(jnp.float32)
    return jnp.einsum('bqhs,bqh->bqs', jax.nn.relu(s), wi.astype(jnp.float32))


def _sparse_attend(q, k_sel, v_sel, valid):
    b, nq = q.shape[:2]
    qg = q.reshape(b, nq, A_KV_HEADS, A_HEADS // A_KV_HEADS, A_HEAD_DIM)
    s = jnp.einsum('bqkgd,bqnkd->bqkgn', qg, k_sel).astype(jnp.float32) * (A_HEAD_DIM ** -0.5)
    s = jnp.where(valid[:, :, None, None, :], s, -jnp.inf)
    p = jax.nn.softmax(s, axis=-1).astype(v_sel.dtype)
    o = jnp.einsum('bqkgn,bqnkd->bqkgd', p, v_sel)
    return o.reshape(b, nq, A_HEADS * A_HEAD_DIM)


def _gather_rows(rows, idx):
    return jax.vmap(lambda r, i: r[i])(rows, idx)


def _dsa_prompt(h, w_in, w_out):
    b, t, _ = h.shape
    pos = jnp.arange(t)
    q, k, v, qi, ki, wi = _proj_a(h, w_in, pos)
    n_sel = min(TOPK_MAX, t // 4)

    def block(qb):
        start = qb * QUERY_BLOCK
        qpos = start + jnp.arange(QUERY_BLOCK)
        qs = lax.dynamic_slice_in_dim(q, start, QUERY_BLOCK, axis=1)
        qis = lax.dynamic_slice_in_dim(qi, start, QUERY_BLOCK, axis=1)
        wis = lax.dynamic_slice_in_dim(wi, start, QUERY_BLOCK, axis=1)
        score = _index_scores(qis, wis, ki)
        score = jnp.where(pos[None, None, :] <= qpos[None, :, None], score, -jnp.inf)
        _, sel = lax.top_k(score, n_sel)
        valid = sel <= qpos[None, :, None]
        return _sparse_attend(qs, _gather_rows(k, sel), _gather_rows(v, sel), valid)

    o = lax.map(block, jnp.arange(t // QUERY_BLOCK))
    o = jnp.moveaxis(o, 0, 1).reshape(b, t, A_HEADS * A_HEAD_DIM)
    return o @ w_out, k, v, ki


def _dsa_sample(h, w_in, w_out, ck, cv, cki, layer, page_table):
    b, s_new, _ = h.shape
    past = PAST_LEN
    pos = past + jnp.arange(s_new)
    q, k, v, qi, ki, wi = _proj_a(h, w_in, pos)
    ki_past = cki[layer, page_table].reshape(b, past, IDX_DIM)
    ki_all = jnp.concatenate([ki_past, ki], axis=1)
    n_keys = past + s_new
    n_sel = min(TOPK_MAX, n_keys // 4)
    score = _index_scores(qi, wi, ki_all)
    kpos = jnp.arange(n_keys)
    score = jnp.where(kpos[None, None, :] <= pos[None, :, None], score, -jnp.inf)
    _, sel = lax.top_k(score, n_sel)
    in_past = (sel < past)[..., None, None]
    psel = jnp.minimum(sel, past - 1)
    phys = jax.vmap(lambda pt, i: pt[i])(page_table, psel // PAGE_SIZE)
    row = psel % PAGE_SIZE
    nsel = jnp.clip(sel - past, 0, s_new - 1)
    k_sel = jnp.where(in_past, ck[layer, phys, row], _gather_rows(k, nsel))
    v_sel = jnp.where(in_past, cv[layer, phys, row], _gather_rows(v, nsel))
    valid = sel <= pos[None, :, None]
    o = _sparse_attend(q, k_sel, v_sel, valid)
    return o @ w_out, k, v, ki


def _proj_b(h, w_in, pos):
    b, t, _ = h.shape
    q, k, v = _split(h @ w_in, B_SPLITS)
    q = _rope(q.reshape(b, t, B_HEADS, B_HEAD_DIM), pos)
    k = _rope(k.reshape(b, t, B_KV_HEADS, B_HEAD_DIM), pos)
    v = v.reshape(b, t, B_KV_HEADS, B_HEAD_DIM)
    return q, k, v


def _sink_attend(q, k, v, mask, sinks):
    b, nq = q.shape[:2]
    g = B_HEADS // B_KV_HEADS
    qg = q.reshape(b, nq, B_KV_HEADS, g, B_HEAD_DIM)
    s = jnp.einsum('bqkgd,bnkd->bqkgn', qg, k).astype(jnp.float32) * (B_HEAD_DIM ** -0.5)
    s = jnp.where(mask[None, :, None, None, :], s, -jnp.inf)
    sink = jnp.broadcast_to(sinks.astype(jnp.float32).reshape(1, 1, B_KV_HEADS, g, 1), s.shape[:-1] + (1,))
    p = jax.nn.softmax(jnp.concatenate([s, sink], axis=-1), axis=-1)[..., :-1]
    o = jnp.einsum('bqkgn,bnkd->bqkgd', p.astype(v.dtype), v)
    return o.reshape(b, nq, B_HEADS * B_HEAD_DIM)


def _swa_prompt(h, w_in, w_out, sinks):
    b, t, _ = h.shape
    pos = jnp.arange(t)
    q, k, v = _proj_b(h, w_in, pos)
    nb = t // WINDOW

    def band(x):
        xb = x.reshape(b, nb, WINDOW, B_KV_HEADS, B_HEAD_DIM)
        prev = jnp.concatenate([jnp.zeros_like(xb[:, :1]), xb[:, :-1]], axis=1)
        return jnp.moveaxis(jnp.concatenate([prev, xb], axis=2), 1, 0)

    qb = jnp.moveaxis(q.reshape(b, nb, WINDOW, B_HEADS, B_HEAD_DIM), 1, 0)
    off = jnp.arange(WINDOW)[:, None] - (jnp.arange(2 * WINDOW)[None, :] - WINDOW)

    def block(args):
        i, qi, ki, vi = args
        kpos = (i - 1) * WINDOW + jnp.arange(2 * WINDOW)
        mask = (off >= 0) & (off < WINDOW) & (kpos[None, :] >= 0)
        return _sink_attend(qi, ki, vi, mask, sinks)

    o = lax.map(block, (jnp.arange(nb), qb, band(k), band(v)))
    o = jnp.moveaxis(o, 0, 1).reshape(b, t, B_HEADS * B_HEAD_DIM)
    buf = min(WINDOW, t)
    return o @ w_out, k[:, t - buf:], v[:, t - buf:]


def _swa_sample(h, w_in, w_out, sinks, kbuf, vbuf):
    b, s_new, _ = h.shape
    pos = PAST_LEN + jnp.arange(s_new)
    q, k, v = _proj_b(h, w_in, pos)
    wb = kbuf.shape[1]
    k_all = jnp.concatenate([kbuf, k], axis=1)
    v_all = jnp.concatenate([vbuf, v], axis=1)
    kpos = PAST_LEN - wb + jnp.arange(wb + s_new)
    dist = pos[:, None] - kpos[None, :]
    mask = (dist >= 0) & (dist < WINDOW)
    o = _sink_attend(q, k_all, v_all, mask, sinks)
    return o @ w_out, k_all[:, s_new:], v_all[:, s_new:]


def _swiglu(x, wg, wu, wd):
    return (jax.nn.silu(x @ wg) * (x @ wu)) @ wd


def _moe(x, w_router, wg, wu, wd):
    logits = (x @ w_router).astype(jnp.float32)
    top_val, top_idx = lax.top_k(logits, TOP_K_EXPERTS)
    gates = jax.nn.softmax(top_val, axis=-1)
    dense_gate = jnp.sum(jax.nn.one_hot(top_idx, N_EXPERTS, dtype=jnp.float32) * gates[..., None], axis=-2)
    out = jnp.zeros(x.shape, jnp.float32)
    for e in range(N_EXPERTS):
        out = out + dense_gate[:, e:e + 1] * _swiglu(x, wg[e], wu[e], wd[e]).astype(jnp.float32)
    return out.astype(x.dtype)


def setup_inputs(seed: int = 0) -> dict:
    key = jax.random.key(seed)
    keys = list(jax.random.split(key, 32))

    def nrm(shape, scale):
        return jax.random.normal(keys.pop(), shape, jnp.float32) * scale

    d = D_MODEL
    n_pages = PAST_LEN // PAGE_SIZE
    n_pool = (5 * DEC_BATCH * n_pages + 3) // 4
    w_buf = min(WINDOW, PAST_LEN)
    inputs = {}
    inputs['x_prompt'] = nrm((BATCH, SEQ, d), 1.0)
    inputs['x_sample'] = nrm((DEC_BATCH, DEC_SEQ, d), 1.0)
    inputs['cache_k_a'] = nrm((N_A, n_pool, PAGE_SIZE, A_KV_HEADS, A_HEAD_DIM), 1.0)
    inputs['cache_v_a'] = nrm((N_A, n_pool, PAGE_SIZE, A_KV_HEADS, A_HEAD_DIM), 1.0)
    inputs['cache_kidx_a'] = nrm((N_A, n_pool, PAGE_SIZE, IDX_DIM), 1.0)
    inputs['state_k_b'] = nrm((N_B, DEC_BATCH, w_buf, B_KV_HEADS, B_HEAD_DIM), 1.0)
    inputs['state_v_b'] = nrm((N_B, DEC_BATCH, w_buf, B_KV_HEADS, B_HEAD_DIM), 1.0)
    perm = jax.random.permutation(keys.pop(), n_pool)
    inputs['page_table'] = perm[:DEC_BATCH * n_pages].reshape(DEC_BATCH, n_pages).astype(jnp.int32)
    inputs['c_prompt'] = nrm((BATCH, d), 1.0)
    inputs['c_sample'] = nrm((DEC_BATCH, d), 1.0)
    inputs['w_ada'] = nrm((DEPTH, d, 6 * d), 0.5 * d ** -0.5)
    inputs['b_ada'] = nrm((DEPTH, 6 * d), 0.02)
    inputs['g_attn'] = 1.0 + nrm((DEPTH, d), 0.02)
    inputs['g_ffn'] = 1.0 + nrm((DEPTH, d), 0.02)
    inputs['w_in_a'] = nrm((N_A, d, A_PROJ), d ** -0.5)
    inputs['w_out_a'] = nrm((N_A, A_HEADS * A_HEAD_DIM, d), (A_HEADS * A_HEAD_DIM) ** -0.5)
    inputs['w_in_b'] = nrm((N_B, d, B_PROJ), d ** -0.5)
    inputs['w_out_b'] = nrm((N_B, B_HEADS * B_HEAD_DIM, d), (B_HEADS * B_HEAD_DIM) ** -0.5)
    inputs['sinks_b'] = nrm((N_B, B_HEADS), 1.0)
    inputs['w_gate_d'] = nrm((N_A, d, FFN_DIM), d ** -0.5)
    inputs['w_up_d'] = nrm((N_A, d, FFN_DIM), d ** -0.5)
    inputs['w_down_d'] = nrm((N_A, FFN_DIM, d), FFN_DIM ** -0.5)
    inputs['w_router'] = nrm((N_B, d, N_EXPERTS), d ** -0.5)
    inputs['w_gate_e'] = nrm((N_B, N_EXPERTS, d, FFN_DIM), d ** -0.5)
    inputs['w_up_e'] = nrm((N_B, N_EXPERTS, d, FFN_DIM), d ** -0.5)
    inputs['w_down_e'] = nrm((N_B, N_EXPERTS, FFN_DIM, d), FFN_DIM ** -0.5)
    inputs['g_final'] = 1.0 + nrm((d,), 0.02)
    return inputs


def reference(x_prompt, x_sample, cache_k_a, cache_v_a, cache_kidx_a, state_k_b, state_v_b, page_table,
              c_prompt, c_sample, w_ada, b_ada, g_attn, g_ffn, w_in_a, w_out_a, w_in_b, w_out_b, sinks_b,
              w_gate_d, w_up_d, w_down_d, w_router, w_gate_e, w_up_e, w_down_e, g_final):
    xp, xs = x_prompt, x_sample
    n_prompt = xp.shape[0] * xp.shape[1]
    ka_p, va_p, ia_p, ka_s, va_s, ia_s = [], [], [], [], [], []
    kb_p, vb_p, kb_s, vb_s = [], [], [], []
    for i in range(DEPTH):
        j = i // 2
        mp = _adaln(c_prompt, w_ada[i], b_ada[i])
        ms = _adaln(c_sample, w_ada[i], b_ada[i])
        hp = _modulate(xp, g_attn[i], mp[0], mp[1])
        hs = _modulate(xs, g_attn[i], ms[0], ms[1])
        if i % 2 == 0:
            op, k_, v_, ki_ = _dsa_prompt(hp, w_in_a[j], w_out_a[j])
            ka_p.append(k_)
            va_p.append(v_)
            ia_p.append(ki_)
            os_, k_, v_, ki_ = _dsa_sample(hs, w_in_a[j], w_out_a[j], cache_k_a, cache_v_a, cache_kidx_a, j, page_table)
            ka_s.append(k_)
            va_s.append(v_)
            ia_s.append(ki_)
        else:
            op, k_, v_ = _swa_prompt(hp, w_in_b[j], w_out_b[j], sinks_b[j])
            kb_p.append(k_)
            vb_p.append(v_)
            os_, k_, v_ = _swa_sample(hs, w_in_b[j], w_out_b[j], sinks_b[j], state_k_b[j], state_v_b[j])
            kb_s.append(k_)
            vb_s.append(v_)
        xp = xp + mp[2][:, None, :] * op
        xs = xs + ms[2][:, None, :] * os_
        hp = _modulate(xp, g_ffn[i], mp[3], mp[4])
        hs = _modulate(xs, g_ffn[i], ms[3], ms[4])
        rows = jnp.concatenate([hp.reshape(-1, D_MODEL), hs.reshape(-1, D_MODEL)], axis=0)
        if i % 2 == 0:
            f = _swiglu(rows, w_gate_d[j], w_up_d[j], w_down_d[j])
        else:
            f = _moe(rows, w_router[j], w_gate_e[j], w_up_e[j], w_down_e[j])
        xp = xp + mp[5][:, None, :] * f[:n_prompt].reshape(xp.shape)
        xs = xs + ms[5][:, None, :] * f[n_prompt:].reshape(xs.shape)
    y_prompt = _rms(xp, g_final)
    y_sample = _rms(xs, g_final)
    k_a_prompt = jnp.stack(ka_p)
    v_a_prompt = jnp.stack(va_p)
    kidx_a_prompt = jnp.stack(ia_p)
    k_a_sample = jnp.stack(ka_s)
    v_a_sample = jnp.stack(va_s)
    kidx_a_sample = jnp.stack(ia_s)
    k_b_prompt = jnp.stack(kb_p)
    v_b_prompt = jnp.stack(vb_p)
    k_b_sample = jnp.stack(kb_s)
    v_b_sample = jnp.stack(vb_s)
    return (y_prompt, y_sample, k_a_prompt, v_a_prompt, kidx_a_prompt, k_a_sample, v_a_sample, kidx_a_sample,
            k_b_prompt, v_b_prompt, k_b_sample, v_b_sample)
```

```python
import functools
from typing import NamedTuple

import numpy as np
import jax
import jax.numpy as jnp
from jax import lax
from jax.experimental import pallas as pl
from jax.experimental.pallas import tpu as pltpu

F32, BF16, I32 = jnp.float32, jnp.bfloat16, jnp.int32

D = 2048
BATCH, SEQ = 4, 2048
DEPTH = 4
DEC_BATCH = 32
PAST = 16384
PAGE = 128
N_PAGES = PAST // PAGE
A_HEADS, A_KV, A_DH = 16, 4, 128
IDX_HEADS, IDX_DIM = 16, 64
TOPK = 256
B_HEADS, B_KV, B_DH = 32, 4, 64
WINDOW = 128
FFN = 7168
N_EXP = 8
ROPE_THETA = 500000.0
EPS = 1e-6
A_PROJ = 4176
A_PROJ_PAD = 4608
B_PROJ = 2560

NP = BATCH * SEQ
SAMPLE_ROWS = 256
RP = NP + SAMPLE_ROWS
N_TOK = NP + DEC_BATCH

VMEM_LIMIT = 56 * 1024 * 1024
LANES = 128

NEG_INF = float("-inf")
INT_MIN = -2147483648
KEY_NEG_INF = -2139095041

FFN_TM = 1024
FFN_CH = 256
FFN_TF = 256
N_CH = FFN_TM // FFN_CH
MOE_TILES = (2 * N_TOK) // FFN_TM + N_EXP
DENSE_TILES = NP // FFN_TM + 1

SCORE_N = (N_PAGES + 1) * PAGE


def _cp(*sem):
    return pltpu.CompilerParams(dimension_semantics=sem, vmem_limit_bytes=VMEM_LIMIT)


class _Rows(NamedTuple):
    tm: int
    n: int
    off: int
    mod: jax.Array
    per_row: bool


def _mod_spec(rows, chunk, width=D, col=None):
    per_blk = width if col is not None else D
    nblk = D // per_blk

    def idx(i, *rest):
        j = rest[0] if col is not None else 0
        c = chunk * nblk + j
        if rows.per_row:
            return (0, 0, c)
        return ((i * rows.tm) // SEQ, 0, c)

    shape = (1, rows.tm if rows.per_row else 1, per_blk)
    return pl.BlockSpec(shape, idx)


def _adaln_kernel(c_ref, w_ref, b_ref, o_ref):
    c = c_ref[...]
    a = (c / (1.0 + jnp.exp(-c))).astype(BF16)
    o_ref[0] = jnp.dot(a, w_ref[0].astype(BF16), preferred_element_type=F32) + b_ref[0]


def _adaln(c_all, w_ada, b_ada):
    tn = 1024
    rows = c_all.shape[0]
    return pl.pallas_call(
        _adaln_kernel,
        grid=(DEPTH, 6 * D // tn),
        in_specs=[pl.BlockSpec((rows, D), lambda l, j: (0, 0)),
                  pl.BlockSpec((1, D, tn), lambda l, j: (l, 0, j)),
                  pl.BlockSpec((1, 1, tn), lambda l, j: (l, 0, j))],
        out_specs=pl.BlockSpec((1, rows, tn), lambda l, j: (l, 0, j)),
        out_shape=jax.ShapeDtypeStruct((DEPTH, rows, 6 * D), F32),
        compiler_params=_cp("arbitrary", "arbitrary"),
        name="adaln",
    )(c_all, w_ada, b_ada.reshape(DEPTH, 1, 6 * D))


T_ROPE128, T_PLAIN, T_ROPE64, T_KIWI = 0, 1, 2, 3


def _rope_tables(pos):
    pos = jnp.asarray(pos, F32)[:, None]
    col = np.arange(LANES)

    def kind(head_dim, ncols):
        rot = head_dim // 4
        half = rot // 2
        inv = ROPE_THETA ** (-jnp.arange(half, dtype=F32) * 2.0 / rot)
        ang = pos * inv[None, :]
        cos, sin = jnp.cos(ang), jnp.sin(ang)
        c = col % head_dim
        lo = (c < half) & (col < ncols)
        hi = (c >= half) & (c < rot) & (col < ncols)
        f = np.where(lo, c, np.where(hi, c - half, 0))
        one = (col < ncols).astype(np.float32)
        cc = jnp.where(lo | hi, cos[:, f], one[None, :])
        aa = jnp.where(lo, -sin[:, f], 0.0)
        bb = jnp.where(hi, sin[:, f], 0.0)
        return cc, aa, bb

    k128 = kind(A_DH, LANES)
    k64 = kind(IDX_DIM, LANES)
    plain = (jnp.ones_like(k128[0]), jnp.zeros_like(k128[0]), jnp.zeros_like(k128[0]))
    kw = list(kind(IDX_DIM, IDX_DIM))
    wi_cols = (col >= IDX_DIM) & (col < IDX_DIM + IDX_HEADS)
    kw[0] = jnp.where(wi_cols[None, :], IDX_HEADS ** -0.5 * IDX_DIM ** -0.5, kw[0])
    kinds = [k128, plain, k64, tuple(kw)]
    return tuple(jnp.stack([k[t] for k in kinds]) for t in range(3))


def _modulated(x, g, shift, scale):
    ms = jnp.mean(x * x, axis=-1, keepdims=True)
    return (x * lax.rsqrt(ms + EPS) * g) * (1.0 + scale) + shift


def _proj_kernel(tt_ref, x_ref, g_ref, sh_ref, sc_ref, w_ref, c_ref, a_ref, b_ref, o_ref, h_ref, *, tn):
    j = pl.program_id(1)

    @pl.when(j == 0)
    def _():
        h_ref[...] = _modulated(x_ref[...], g_ref[...], sh_ref[0], sc_ref[0]).astype(BF16)

    y = jnp.dot(h_ref[...], w_ref[...].astype(BF16), preferred_element_type=F32)
    t = tt_ref[j]
    reps = tn // LANES

    def tab(r):
        return jnp.concatenate([r[0]] * reps, axis=1)

    @pl.when(t == T_PLAIN)
    def _():
        o_ref[...] = y

    def rope(half):
        o_ref[...] = (y * tab(c_ref) + pltpu.roll(y, tn - half, 1) * tab(a_ref)
                      + pltpu.roll(y, half, 1) * tab(b_ref))

    @pl.when(t == T_ROPE128)
    def _():
        rope(A_DH // 8)

    @pl.when(t >= T_ROPE64)
    def _():
        rope(IDX_DIM // 8)


def _proj(x, rows, g, shift_chunk, w, tables, tile_types, tn, name):
    tm = rows.tm
    nj = w.shape[1] // tn
    tt = jnp.asarray(np.asarray(tile_types, np.int32))
    pos_blocks = tables[0].shape[1] // tm
    tab_spec = pl.BlockSpec((1, tm, LANES), lambda i, j, tt: (tt[j], i % pos_blocks, 0))
    grid_spec = pltpu.PrefetchScalarGridSpec(
        num_scalar_prefetch=1,
        grid=(rows.n, nj),
        in_specs=[pl.BlockSpec((tm, D), lambda i, j, tt: (i + rows.off, 0)),
                  pl.BlockSpec((1, D), lambda i, j, tt: (0, 0)),
                  _mod_spec(rows, shift_chunk),
                  _mod_spec(rows, shift_chunk + 1),
                  pl.BlockSpec((D, tn), lambda i, j, tt: (0, j)),
                  tab_spec, tab_spec, tab_spec],
        out_specs=pl.BlockSpec((tm, tn), lambda i, j, tt: (i, j)),
        scratch_shapes=[pltpu.VMEM((tm, D), BF16)])
    return pl.pallas_call(
        functools.partial(_proj_kernel, tn=tn),
        grid_spec=grid_spec,
        out_shape=jax.ShapeDtypeStruct((rows.n * tm, nj * tn), F32),
        compiler_params=_cp("arbitrary", "arbitrary"),
        name=name,
    )(tt, x, g.reshape(1, D), rows.mod, rows.mod, w, *tables)


def _ordered_key(x):
    b = pltpu.bitcast(x, I32)
    return b ^ ((b >> 31) & 0x7FFFFFFF)


def _select_topk(score, valid, pos, k, idx_bits, j_ref):
    rows = score.shape[0]
    key = jnp.where(valid, _ordered_key(score), KEY_NEG_INF)

    def count(m):
        return jnp.sum(jnp.where(m, 1.0, 0.0), axis=1, keepdims=True)

    def value_step(i, cand):
        trial = cand + jnp.left_shift(jnp.int32(1), 31 - i)
        return jnp.where(count(key >= trial) >= k, trial, cand)

    thr = lax.fori_loop(0, 32, value_step, jnp.full((rows, 1), INT_MIN, I32))
    gt = key > thr
    eq = key == thr
    need = k - count(gt)
    tied = jnp.where(count(eq) > need, jnp.where(thr > KEY_NEG_INF, 1.0, 0.0), 0.0)
    j_ref[...] = jnp.full((rows, 1), 1 << idx_bits, I32)

    @pl.when(jnp.max(tied) > 0.0)
    def _():
        def index_step(i, j):
            trial = j + jnp.left_shift(jnp.int32(1), idx_bits - 1 - i)
            below = jnp.where(eq, jnp.where(pos < trial, 1.0, 0.0), 0.0)
            return jnp.where(jnp.sum(below, axis=1, keepdims=True) < need, trial, j)

        j_ref[...] = lax.fori_loop(0, idx_bits, index_step, jnp.zeros((rows, 1), I32))

    return valid & (gt | (eq & (pos <= j_ref[...])))


def _dsa_prompt_kernel(q_ref, qi_ref, wq_ref, k_ref, v_ref, kw_ref, o_ref, kb, vb, kib, j_ref, *, tq):
    qb = pl.program_id(1)

    @pl.when(qb == 0)
    def _():
        kb[...] = k_ref[...].astype(BF16)
        vb[...] = v_ref[...].astype(BF16)
        kib[...] = kw_ref[:, 0:IDX_DIM].astype(BF16)

    ki = kib[...]
    score = jnp.zeros((tq, SEQ), F32)
    for h in range(IDX_HEADS):
        qh = qi_ref[:, h * IDX_DIM:(h + 1) * IDX_DIM].astype(BF16)
        s = lax.dot_general(qh, ki, (((1,), (1,)), ((), ())), preferred_element_type=F32)
        score = score + wq_ref[:, IDX_DIM + h:IDX_DIM + h + 1] * jnp.maximum(s, 0.0)

    kpos = lax.broadcasted_iota(I32, (tq, SEQ), 1)
    qpos = qb * tq + lax.broadcasted_iota(I32, (tq, SEQ), 0)
    sel = _select_topk(score, kpos <= qpos, kpos, min(TOPK, SEQ // 4), 11, j_ref)
    bias = jnp.where(sel, 0.0, NEG_INF)

    for h in range(A_HEADS):
        kv = h // (A_HEADS // A_KV)
        qh = q_ref[:, h * A_DH:(h + 1) * A_DH].astype(BF16)
        s = lax.dot_general(qh, kb[:, kv * A_DH:(kv + 1) * A_DH], (((1,), (1,)), ((), ())),
                            preferred_element_type=F32) * (A_DH ** -0.5) + bias
        m = jnp.max(s, axis=1, keepdims=True)
        p = jnp.exp(s - m)
        l = jnp.sum(p, axis=1, keepdims=True)
        o = jnp.dot(p.astype(BF16), vb[:, kv * A_DH:(kv + 1) * A_DH], preferred_element_type=F32) / l
        o_ref[:, h * A_DH:(h + 1) * A_DH] = o.astype(BF16)


def _dsa_prompt(pa):
    tq = 128
    nq = SEQ // tq
    kcol = 512
    return pl.pallas_call(
        functools.partial(_dsa_prompt_kernel, tq=tq),
        grid=(BATCH, nq),
        in_specs=[pl.BlockSpec((tq, D), lambda b, i: (b * nq + i, 0)),
                  pl.BlockSpec((tq, 1024), lambda b, i: (b * nq + i, 3)),
                  pl.BlockSpec((tq, kcol), lambda b, i: (b * nq + i, 8)),
                  pl.BlockSpec((SEQ, kcol), lambda b, i: (b, 4)),
                  pl.BlockSpec((SEQ, kcol), lambda b, i: (b, 5)),
                  pl.BlockSpec((SEQ, kcol), lambda b, i: (b, 8))],
        out_specs=pl.BlockSpec((tq, D), lambda b, i: (b * nq + i, 0)),
        out_shape=jax.ShapeDtypeStruct((NP, D), BF16),
        scratch_shapes=[pltpu.VMEM((SEQ, kcol), BF16), pltpu.VMEM((SEQ, kcol), BF16),
                        pltpu.VMEM((SEQ, IDX_DIM), BF16), pltpu.VMEM((tq, 1), I32)],
        compiler_params=_cp("arbitrary", "arbitrary"),
        name="dsa_prompt",
    )(pa, pa, pa, pa, pa, pa)


def _dsa_score_kernel(pt_ref, qi_ref, wi_ref, kin_ref, cki_hbm, o_ref, kibuf, sem, *, layer):
    b = pl.program_id(0)

    def page_copy(j):
        return pltpu.make_async_copy(cki_hbm.at[layer, pt_ref[b, j]], kibuf.at[j], sem)

    def issue(j, c):
        page_copy(j).start()
        return c

    lax.fori_loop(0, N_PAGES, issue, 0)
    row = lax.broadcasted_iota(I32, (PAGE, IDX_DIM), 0)
    kibuf[N_PAGES] = jnp.where(row == 0, kin_ref[0], 0.0)

    def wait(j, c):
        page_copy(j).wait()
        return c

    lax.fori_loop(0, N_PAGES, wait, 0)
    kflat = kibuf[...].reshape(SCORE_N, IDX_DIM).astype(BF16)
    s = lax.dot_general(qi_ref[0].astype(BF16), kflat, (((1,), (1,)), ((), ())),
                        preferred_element_type=F32)
    o_ref[0] = jnp.sum(wi_ref[0] * jnp.maximum(s, 0.0), axis=0, keepdims=True)


def _dsa_score(page_table, qi, wi, ki_new, cache_kidx, layer):
    grid_spec = pltpu.PrefetchScalarGridSpec(
        num_scalar_prefetch=1,
        grid=(DEC_BATCH,),
        in_specs=[pl.BlockSpec((1, IDX_HEADS, IDX_DIM), lambda b, pt: (b, 0, 0)),
                  pl.BlockSpec((1, IDX_HEADS, 1), lambda b, pt: (b, 0, 0)),
                  pl.BlockSpec((1, 1, IDX_DIM), lambda b, pt: (b, 0, 0)),
                  pl.BlockSpec(memory_space=pl.ANY)],
        out_specs=pl.BlockSpec((1, 1, SCORE_N), lambda b, pt: (b, 0, 0)),
        scratch_shapes=[pltpu.VMEM((N_PAGES + 1, PAGE, IDX_DIM), F32), pltpu.SemaphoreType.DMA])
    out = pl.pallas_call(
        functools.partial(_dsa_score_kernel, layer=layer),
        grid_spec=grid_spec,
        out_shape=jax.ShapeDtypeStruct((DEC_BATCH, 1, SCORE_N), F32),
        compiler_params=_cp("arbitrary"),
        name="dsa_sample_score",
    )(page_table, qi, wi, ki_new, cache_kidx)
    return out.reshape(DEC_BATCH, SCORE_N)


def _dsa_select_kernel(s_ref, o_ref, j_ref, slot_ref):
    n_groups = SCORE_N // LANES
    pos = lax.broadcasted_iota(I32, (DEC_BATCH, SCORE_N), 1)
    sel = _select_topk(s_ref[...], pos <= PAST, pos, TOPK, 15, j_ref)
    self_ = jnp.where(sel, 1.0, 0.0).astype(BF16)

    gp = lax.broadcasted_iota(I32, (SCORE_N, 2 * LANES), 0) // LANES
    gg = lax.broadcasted_iota(I32, (SCORE_N, 2 * LANES), 1)
    before = jnp.dot(self_, jnp.where(gp < gg, 1.0, 0.0).astype(BF16), preferred_element_type=F32)
    tri = jnp.where(lax.broadcasted_iota(I32, (LANES, LANES), 0) <= lax.broadcasted_iota(I32, (LANES, LANES), 1),
                    1.0, 0.0).astype(BF16)
    for g in range(n_groups):
        sl = slice(g * LANES, (g + 1) * LANES)
        within = jnp.dot(self_[:, sl], tri, preferred_element_type=F32)
        slot = before[:, g:g + 1] + within - 1.0
        slot_ref[:, sl] = jnp.where(sel[:, sl], slot, -1.0)

    srow = lax.broadcasted_iota(I32, (TOPK, LANES), 0).astype(F32)
    lane = lax.broadcasted_iota(I32, (TOPK, LANES), 1)

    out = jnp.zeros((TOPK, LANES), F32)
    for b in range(DEC_BATCH):
        def per_group(g, acc):
            off = pl.multiple_of(g * LANES, LANES)
            sl = slot_ref[b:b + 1, pl.ds(off, LANES)]
            return acc + jnp.where(sl == srow, (off + lane).astype(F32), 0.0)

        acc = lax.fori_loop(0, n_groups, per_group, jnp.zeros((TOPK, LANES), F32))
        out = jnp.where(lane == b, jnp.sum(acc, axis=1, keepdims=True), out)
    o_ref[...] = out


def _dsa_select(scores):
    out = pl.pallas_call(
        _dsa_select_kernel,
        grid=(1,),
        in_specs=[pl.BlockSpec((DEC_BATCH, SCORE_N), lambda i: (0, 0))],
        out_specs=pl.BlockSpec((TOPK, LANES), lambda i: (0, 0)),
        out_shape=jax.ShapeDtypeStruct((TOPK, LANES), F32),
        scratch_shapes=[pltpu.VMEM((DEC_BATCH, 1), I32), pltpu.VMEM((DEC_BATCH, SCORE_N), F32)],
        compiler_params=_cp("arbitrary"),
        name="dsa_sample_select",
    )(scores)
    return out[:, :DEC_BATCH].T.astype(I32)


def _dsa_attend_kernel(sel_ref, pt_ref, q_ref, kn_hbm, vn_hbm, ck_hbm, cv_hbm, o_ref, kbuf, vbuf, sem, *, layer):
    b = pl.program_id(0)

    def slot_rows(s):
        return pl.ds(pl.multiple_of(s * A_KV, A_KV), A_KV)

    def issue(s, c):
        pos = sel_ref[b, s]

        @pl.when(pos < PAST)
        def _():
            phys = pt_ref[b, pos // PAGE]
            row = pos % PAGE
            pltpu.make_async_copy(ck_hbm.at[layer, phys, row], kbuf.at[slot_rows(s)], sem.at[0]).start()
            pltpu.make_async_copy(cv_hbm.at[layer, phys, row], vbuf.at[slot_rows(s)], sem.at[1]).start()

        @pl.when(pos >= PAST)
        def _():
            pltpu.make_async_copy(kn_hbm.at[b], kbuf.at[slot_rows(s)], sem.at[0]).start()
            pltpu.make_async_copy(vn_hbm.at[b], vbuf.at[slot_rows(s)], sem.at[1]).start()

        return c

    lax.fori_loop(0, TOPK, issue, 0)

    def wait(s, c):
        pltpu.make_async_copy(kn_hbm.at[b], kbuf.at[slot_rows(s)], sem.at[0]).wait()
        pltpu.make_async_copy(vn_hbm.at[b], vbuf.at[slot_rows(s)], sem.at[1]).wait()
        return c

    lax.fori_loop(0, TOPK, wait, 0)

    n = TOPK * A_KV
    s = lax.dot_general(q_ref[0].astype(BF16), kbuf[...].astype(BF16), (((1,), (1,)), ((), ())),
                        preferred_element_type=F32) * (A_DH ** -0.5)
    col = lax.broadcasted_iota(I32, (A_HEADS, n), 1)
    head = lax.broadcasted_iota(I32, (A_HEADS, n), 0)
    s = jnp.where(col % A_KV == head // (A_HEADS // A_KV), s, NEG_INF)
    m = jnp.max(s, axis=1, keepdims=True)
    p = jnp.exp(s - m)
    l = jnp.sum(p, axis=1, keepdims=True)
    o_ref[0] = jnp.dot(p.astype(BF16), vbuf[...].astype(BF16), preferred_element_type=F32) / l


def _dsa_attend(sel, page_table, q, k_new, v_new, cache_k, cache_v, layer):
    grid_spec = pltpu.PrefetchScalarGridSpec(
        num_scalar_prefetch=2,
        grid=(DEC_BATCH,),
        in_specs=[pl.BlockSpec((1, A_HEADS, A_DH), lambda b, s, pt: (b, 0, 0)),
                  pl.BlockSpec(memory_space=pl.ANY), pl.BlockSpec(memory_space=pl.ANY),
                  pl.BlockSpec(memory_space=pl.ANY), pl.BlockSpec(memory_space=pl.ANY)],
        out_specs=pl.BlockSpec((1, A_HEADS, A_DH), lambda b, s, pt: (b, 0, 0)),
        scratch_shapes=[pltpu.VMEM((TOPK * A_KV, A_DH), F32), pltpu.VMEM((TOPK * A_KV, A_DH), F32),
                        pltpu.SemaphoreType.DMA((2,))])
    return pl.pallas_call(
        functools.partial(_dsa_attend_kernel, layer=layer),
        grid_spec=grid_spec,
        out_shape=jax.ShapeDtypeStruct((DEC_BATCH, A_HEADS, A_DH), F32),
        compiler_params=_cp("arbitrary"),
        name="dsa_sample_attend",
    )(sel, page_table, q, k_new, v_new, cache_k, cache_v)


def _sink_softmax_pv(s, sink, v):
    m = jnp.maximum(jnp.max(s, axis=1, keepdims=True), sink)
    p = jnp.exp(s - m)
    den = jnp.sum(p, axis=1, keepdims=True) + jnp.exp(sink - m)
    return jnp.dot(p.astype(BF16), v, preferred_element_type=F32) / den


def _swa_prompt_kernel(q_ref, kp_ref, kc_ref, vp_ref, vc_ref, sink_ref, o_ref):
    qb = pl.program_id(1)
    k = jnp.concatenate([kp_ref[...], kc_ref[...]], axis=0).astype(BF16)
    v = jnp.concatenate([vp_ref[...], vc_ref[...]], axis=0).astype(BF16)
    qrow = lax.broadcasted_iota(I32, (WINDOW, 2 * WINDOW), 0)
    kcol = lax.broadcasted_iota(I32, (WINDOW, 2 * WINDOW), 1)
    first = jnp.where(qb > 0, 0, WINDOW)
    ok = (kcol > qrow) & (kcol <= qrow + WINDOW) & (kcol >= first)
    bias = jnp.where(ok, 0.0, NEG_INF)
    g = B_HEADS // B_KV
    for h in range(B_HEADS):
        kv = h // g
        qh = q_ref[:, h * B_DH:(h + 1) * B_DH].astype(BF16)
        s = lax.dot_general(qh, k[:, kv * B_DH:(kv + 1) * B_DH], (((1,), (1,)), ((), ())),
                            preferred_element_type=F32) * (B_DH ** -0.5) + bias
        o = _sink_softmax_pv(s, sink_ref[h:h + 1, :], v[:, kv * B_DH:(kv + 1) * B_DH])
        o_ref[:, h * B_DH:(h + 1) * B_DH] = o.astype(BF16)


def _swa_prompt(pb, sinks):
    nq = SEQ // WINDOW
    kvw = B_KV * B_DH
    kblk, vblk = D // kvw, D // kvw + 1

    def prev(b, i):
        return b * nq + jnp.maximum(i - 1, 0)

    return pl.pallas_call(
        _swa_prompt_kernel,
        grid=(BATCH, nq),
        in_specs=[pl.BlockSpec((WINDOW, D), lambda b, i: (b * nq + i, 0)),
                  pl.BlockSpec((WINDOW, kvw), lambda b, i: (prev(b, i), kblk)),
                  pl.BlockSpec((WINDOW, kvw), lambda b, i: (b * nq + i, kblk)),
                  pl.BlockSpec((WINDOW, kvw), lambda b, i: (prev(b, i), vblk)),
                  pl.BlockSpec((WINDOW, kvw), lambda b, i: (b * nq + i, vblk)),
                  pl.BlockSpec((B_HEADS, 1), lambda b, i: (0, 0))],
        out_specs=pl.BlockSpec((WINDOW, D), lambda b, i: (b * nq + i, 0)),
        out_shape=jax.ShapeDtypeStruct((NP, D), BF16),
        compiler_params=_cp("arbitrary", "arbitrary"),
        name="swa_prompt",
    )(pb, pb, pb, pb, pb, sinks.reshape(B_HEADS, 1))


def _swa_sample_kernel(q_ref, kn_ref, vn_ref, kbuf_ref, vbuf_ref, sink_ref, o_ref, ko_ref, vo_ref):
    row = lax.broadcasted_iota(I32, (WINDOW, B_KV * B_DH), 0)
    k = jnp.where(row == WINDOW - 1, kn_ref[0], pltpu.roll(kbuf_ref[0], WINDOW - 1, 0))
    v = jnp.where(row == WINDOW - 1, vn_ref[0], pltpu.roll(vbuf_ref[0], WINDOW - 1, 0))
    ko_ref[0] = k
    vo_ref[0] = v
    kb, vb = k.astype(BF16), v.astype(BF16)
    g = B_HEADS // B_KV
    for kv in range(B_KV):
        qg = q_ref[0, kv * g:(kv + 1) * g, :].astype(BF16)
        s = lax.dot_general(qg, kb[:, kv * B_DH:(kv + 1) * B_DH], (((1,), (1,)), ((), ())),
                            preferred_element_type=F32) * (B_DH ** -0.5)
        o_ref[0, kv * g:(kv + 1) * g, :] = _sink_softmax_pv(s, sink_ref[kv * g:(kv + 1) * g, :],
                                                            vb[:, kv * B_DH:(kv + 1) * B_DH])


def _swa_sample(q, k_new, v_new, kbuf, vbuf, sinks):
    kvw = B_KV * B_DH
    state = jax.ShapeDtypeStruct((DEC_BATCH, WINDOW, kvw), F32)
    per_b3 = lambda shape: pl.BlockSpec((1,) + shape, lambda b: (b, 0, 0))
    return pl.pallas_call(
        _swa_sample_kernel,
        grid=(DEC_BATCH,),
        in_specs=[per_b3((B_HEADS, B_DH)), per_b3((1, kvw)), per_b3((1, kvw)),
                  per_b3((WINDOW, kvw)), per_b3((WINDOW, kvw)),
                  pl.BlockSpec((B_HEADS, 1), lambda b: (0, 0))],
        out_specs=[per_b3((B_HEADS, B_DH)), per_b3((WINDOW, kvw)), per_b3((WINDOW, kvw))],
        out_shape=[jax.ShapeDtypeStruct((DEC_BATCH, B_HEADS, B_DH), F32), state, state],
        compiler_params=_cp("arbitrary"),
        name="swa_sample",
    )(q, k_new, v_new, kbuf, vbuf, sinks.reshape(B_HEADS, 1))


def _out_proj_kernel(o_ref, w_ref, x_ref, gate_ref, xo_ref):
    y = jnp.dot(o_ref[...], w_ref[...].astype(BF16), preferred_element_type=F32)
    xo_ref[...] = x_ref[...] + gate_ref[0] * y


def _out_proj(x, o, rows, w, name):
    tm, tn = rows.tm, 512
    return pl.pallas_call(
        _out_proj_kernel,
        grid=(rows.n, D // tn),
        in_specs=[pl.BlockSpec((tm, D), lambda i, j: (i, 0)),
                  pl.BlockSpec((D, tn), lambda i, j: (0, j)),
                  pl.BlockSpec((tm, tn), lambda i, j: (i + rows.off, j)),
                  _mod_spec(rows, 2, width=tn, col=True)],
        out_specs=pl.BlockSpec((tm, tn), lambda i, j: (i + rows.off, j)),
        out_shape=jax.ShapeDtypeStruct((RP, D), F32),
        input_output_aliases={2: 0},
        compiler_params=_cp("arbitrary", "arbitrary"),
        name=name,
    )(o, w, x, rows.mod)


def _ffn_pre_kernel(x_ref, g_ref, sh_ref, sc_ref, wr_ref, *out_refs, moe):
    h = _modulated(x_ref[...], g_ref[...], sh_ref[0], sc_ref[0])
    out_refs[0][...] = h
    if not moe:
        return
    logits = jnp.dot(h, wr_ref[...], preferred_element_type=F32, precision=lax.Precision.HIGHEST)
    lane = lax.broadcasted_iota(I32, logits.shape, 1).astype(F32)
    l1 = jnp.where(lane < N_EXP, logits, NEG_INF)
    m1 = jnp.max(l1, axis=1, keepdims=True)
    i1 = jnp.min(jnp.where(l1 == m1, lane, float(LANES)), axis=1, keepdims=True)
    l2 = jnp.where(lane == i1, NEG_INF, l1)
    m2 = jnp.max(l2, axis=1, keepdims=True)
    i2 = jnp.min(jnp.where(l2 == m2, lane, float(LANES)), axis=1, keepdims=True)
    e2 = jnp.exp(m2 - m1)
    g1 = 1.0 / (1.0 + e2)
    g2 = e2 / (1.0 + e2)
    out_refs[1][...] = jnp.where(lane == 0, i1, jnp.where(lane == 1, i2, jnp.where(
        lane == 2, g1, jnp.where(lane == 3, g2, 0.0))))


def _ffn_pre(x, rows, g, wr, prev):
    moe = wr is not None
    tm = rows.tm
    row_blk = lambda w: pl.BlockSpec((tm, w), lambda i: (i + rows.off, 0))
    in_specs = [row_blk(D), pl.BlockSpec((1, D), lambda i: (0, 0)), _mod_spec(rows, 3), _mod_spec(rows, 4),
                pl.BlockSpec((D, LANES), lambda i: (0, 0))]
    out_specs = [row_blk(D)] + ([row_blk(LANES)] if moe else [])
    out_shape = [jax.ShapeDtypeStruct((RP, D), F32)] + ([jax.ShapeDtypeStruct((RP, LANES), F32)] if moe else [])
    args = [x, g.reshape(1, D), rows.mod, rows.mod, wr if moe else jnp.zeros((D, LANES), F32)]
    aliases = {}
    if prev is not None:
        in_specs += [pl.BlockSpec(memory_space=pl.ANY)] * len(prev)
        aliases = {len(args) + n: n for n in range(len(prev))}
        args += list(prev)
    kern = functools.partial(_ffn_pre_kernel, moe=moe)
    if prev is not None:
        n_prev = len(prev)
        kern = lambda *refs: _ffn_pre_kernel(*refs[:5], *refs[5 + n_prev:], moe=moe)
    return pl.pallas_call(
        kern, grid=(rows.n,), in_specs=in_specs, out_specs=out_specs, out_shape=out_shape,
        input_output_aliases=aliases, compiler_params=_cp("arbitrary"), name="ffn_pre",
    )(*args)


def _moe_plan(route):
    e = route[:N_TOK, :2].astype(I32).reshape(-1)
    oh = (e[:, None] == jnp.arange(N_EXP, dtype=I32)[None, :]).astype(I32)
    csum = jnp.cumsum(oh, axis=0)
    rank = jnp.take_along_axis(csum - oh, e[:, None], axis=1)[:, 0]
    counts = csum[-1]
    ntile = (counts + FFN_TM - 1) // FFN_TM
    tend = jnp.cumsum(ntile)
    tstart = tend - ntile
    pos = tstart[e] * FFN_TM + rank
    src = jnp.zeros((MOE_TILES * FFN_TM,), I32).at[pos].set(jnp.arange(2 * N_TOK, dtype=I32) // 2)
    t = jnp.arange(MOE_TILES, dtype=I32)
    te = jnp.minimum(jnp.searchsorted(tend, t, side="right").astype(I32), N_EXP - 1)
    total = tend[-1]
    nrows = jnp.where(t < total, jnp.clip(counts[te] - (t - tstart[te]) * FFN_TM, 0, FFN_TM), 0)
    te = jnp.where(t < total, te, te[total - 1])
    p = pos.reshape(N_TOK, 2)
    p = jnp.concatenate([p, jnp.broadcast_to(p[:1], (RP - N_TOK, 2))], axis=0)
    chunk = jnp.arange(MOE_TILES * N_CH, dtype=I32)
    cvalid = ((chunk % N_CH) * FFN_CH < nrows[chunk // N_CH]).astype(I32)
    return dict(src=src.reshape(MOE_TILES * N_CH, 1, FFN_CH), cvalid=cvalid, te=te, nrows=nrows.astype(I32),
                p1=p[:, 0], p2=p[:, 1])


def _moe_gather_kernel(cv_ref, src_ref, h_hbm, xs_hbm, sem):
    c = pl.program_id(0)

    @pl.when(cv_ref[c] > 0)
    def _():
        def row_copy(r, tok):
            return pltpu.make_async_copy(h_hbm.at[pl.ds(tok, 1)], xs_hbm.at[pl.ds(c * FFN_CH + r, 1)], sem)

        def issue(r, k):
            row_copy(r, src_ref[0, 0, r]).start()
            return k

        lax.fori_loop(0, FFN_CH, issue, 0)

        def wait(r, k):
            row_copy(r, 0).wait()
            return k

        lax.fori_loop(0, FFN_CH, wait, 0)


def _moe_gather(h, plan):
    grid_spec = pltpu.PrefetchScalarGridSpec(
        num_scalar_prefetch=1,
        grid=(MOE_TILES * N_CH,),
        in_specs=[pl.BlockSpec((1, 1, FFN_CH), lambda c, cv: (c, 0, 0), memory_space=pltpu.SMEM),
                  pl.BlockSpec(memory_space=pl.ANY)],
        out_specs=pl.BlockSpec(memory_space=pl.ANY),
        scratch_shapes=[pltpu.SemaphoreType.DMA])
    return pl.pallas_call(
        _moe_gather_kernel, grid_spec=grid_spec,
        out_shape=jax.ShapeDtypeStruct((MOE_TILES * FFN_TM, D), F32),
        compiler_params=_cp("arbitrary"), name="moe_gather",
    )(plan["cvalid"], plan["src"], h)


def _ffn_kernel(te_ref, nr_ref, x_ref, wg_ref, wu_ref, wd_ref, o_ref, xb, wgb, wub, wdb):
    t = pl.program_id(0)
    f = pl.program_id(1)
    nrows = nr_ref[t]

    @pl.when(f == 0)
    def _():
        o_ref[...] = jnp.zeros_like(o_ref)
        for c in range(N_CH):
            @pl.when(c * FFN_CH < nrows)
            def _():
                xb[c * FFN_CH:(c + 1) * FFN_CH, :] = x_ref[c * FFN_CH:(c + 1) * FFN_CH, :].astype(BF16)

    @pl.when(nrows > 0)
    def _():
        wgb[...] = wg_ref[0].astype(BF16)
        wub[...] = wu_ref[0].astype(BF16)
        wdb[...] = wd_ref[0].astype(BF16)
        for c in range(N_CH):
            @pl.when(c * FFN_CH < nrows)
            def _():
                rs = slice(c * FFN_CH, (c + 1) * FFN_CH)
                xc = xb[rs, :]
                gate = jnp.dot(xc, wgb[...], preferred_element_type=F32)
                up = jnp.dot(xc, wub[...], preferred_element_type=F32)
                act = ((gate / (1.0 + jnp.exp(-gate))) * up).astype(BF16)
                o_ref[rs, :] += jnp.dot(act, wdb[...], preferred_element_type=F32)


def _ffn(x, te, nrows, wg, wu, wd, n_tiles):
    nf = FFN // FFN_TF

    def fcol(t, f, te, nr):
        return jnp.where(nr[t] > 0, f, nf - 1)

    grid_spec = pltpu.PrefetchScalarGridSpec(
        num_scalar_prefetch=2,
        grid=(n_tiles, nf),
        in_specs=[pl.BlockSpec((FFN_TM, D), lambda t, f, te, nr: (t, 0)),
                  pl.BlockSpec((1, D, FFN_TF), lambda t, f, te, nr: (te[t], 0, fcol(t, f, te, nr))),
                  pl.BlockSpec((1, D, FFN_TF), lambda t, f, te, nr: (te[t], 0, fcol(t, f, te, nr))),
                  pl.BlockSpec((1, FFN_TF, D), lambda t, f, te, nr: (te[t], fcol(t, f, te, nr), 0))],
        out_specs=pl.BlockSpec((FFN_TM, D), lambda t, f, te, nr: (t, 0)),
        scratch_shapes=[pltpu.VMEM((FFN_TM, D), BF16), pltpu.VMEM((D, FFN_TF), BF16),
                        pltpu.VMEM((D, FFN_TF), BF16), pltpu.VMEM((FFN_TF, D), BF16)])
    return pl.pallas_call(
        _ffn_kernel, grid_spec=grid_spec,
        out_shape=jax.ShapeDtypeStruct((n_tiles * FFN_TM, D), F32),
        compiler_params=_cp("arbitrary", "arbitrary"), name="ffn",
    )(te, nrows, x, wg, wu, wd)


def _dense_residual_kernel(x_ref, y_ref, gate_ref, o_ref):
    o_ref[...] = x_ref[...] + gate_ref[0] * y_ref[...]


def _dense_residual(x, y, rows):
    blk = pl.BlockSpec((rows.tm, D), lambda i: (i + rows.off, 0))
    return pl.pallas_call(
        _dense_residual_kernel, grid=(rows.n,),
        in_specs=[blk, blk, _mod_spec(rows, 5)], out_specs=blk,
        out_shape=jax.ShapeDtypeStruct((RP, D), F32), input_output_aliases={0: 0},
        compiler_params=_cp("arbitrary"), name="dense_residual",
    )(x, y, rows.mod)


def _moe_combine_kernel(p1_ref, p2_ref, x_ref, r_ref, gate_ref, y_hbm, o_ref, ybuf, sem, *, tm, off):
    base = (pl.program_id(0) + off) * tm

    def row_copy(r, k, p):
        return pltpu.make_async_copy(y_hbm.at[pl.ds(p, 1)], ybuf.at[k, pl.ds(r, 1)], sem.at[k])

    def issue(r, c):
        row_copy(r, 0, p1_ref[base + r]).start()
        row_copy(r, 1, p2_ref[base + r]).start()
        return c

    lax.fori_loop(0, tm, issue, 0)

    def wait(r, c):
        row_copy(r, 0, 0).wait()
        row_copy(r, 1, 0).wait()
        return c

    lax.fori_loop(0, tm, wait, 0)
    mix = r_ref[:, 2:3] * ybuf[0] + r_ref[:, 3:4] * ybuf[1]
    o_ref[...] = x_ref[...] + gate_ref[0] * mix


def _moe_combine(x, y, route, plan, rows):
    tm = rows.tm
    blk = lambda w: pl.BlockSpec((tm, w), lambda i, p1, p2: (i + rows.off, 0))
    mod = _mod_spec(rows, 5)
    grid_spec = pltpu.PrefetchScalarGridSpec(
        num_scalar_prefetch=2,
        grid=(rows.n,),
        in_specs=[blk(D), blk(LANES), pl.BlockSpec(mod.block_shape, lambda i, p1, p2: mod.index_map(i)),
                  pl.BlockSpec(memory_space=pl.ANY)],
        out_specs=blk(D),
        scratch_shapes=[pltpu.VMEM((2, tm, D), F32), pltpu.SemaphoreType.DMA((2,))])
    return pl.pallas_call(
        functools.partial(_moe_combine_kernel, tm=tm, off=rows.off), grid_spec=grid_spec,
        out_shape=jax.ShapeDtypeStruct((RP, D), F32), input_output_aliases={2: 0},
        compiler_params=_cp("arbitrary"), name="moe_combine",
    )(plan["p1"], plan["p2"], x, route, rows.mod, y)


def _final_norm_kernel(x_ref, g_ref, o_ref):
    x = x_ref[...]
    ms = jnp.mean(x * x, axis=-1, keepdims=True)
    o_ref[...] = x * lax.rsqrt(ms + EPS) * g_ref[...]


def _final_norm(x, g):
    tm = 256
    return pl.pallas_call(
        _final_norm_kernel, grid=(RP // tm,),
        in_specs=[pl.BlockSpec((tm, D), lambda i: (i, 0)), pl.BlockSpec((1, D), lambda i: (0, 0))],
        out_specs=pl.BlockSpec((tm, D), lambda i: (i, 0)),
        out_shape=jax.ShapeDtypeStruct((RP, D), F32),
        compiler_params=_cp("arbitrary"), name="final_norm",
    )(x, g.reshape(1, D))


A_TILES = (T_ROPE128,) * 5 + (T_PLAIN,) + (T_ROPE64,) * 2 + (T_KIWI,)
B_TILES = (T_ROPE64,) * 9 + (T_PLAIN,)


def kernel(x_prompt, x_sample, cache_k_a, cache_v_a, cache_kidx_a, state_k_b, state_v_b, page_table,
           c_prompt, c_sample, w_ada, b_ada, g_attn, g_ffn, w_in_a, w_out_a, w_in_b, w_out_b, sinks_b,
           w_gate_d, w_up_d, w_down_d, w_router, w_gate_e, w_up_e, w_down_e, g_final):
    x = jnp.concatenate([x_prompt.reshape(NP, D), x_sample.reshape(DEC_BATCH, D),
                         jnp.zeros((SAMPLE_ROWS - DEC_BATCH, D), F32)], axis=0)
    c_all = jnp.concatenate([c_prompt, c_sample, jnp.zeros((4, D), F32)], axis=0)
    mods = _adaln(c_all, w_ada, b_ada)
    mod_p = mods[:, :BATCH].reshape(DEPTH, BATCH, 1, 6 * D)
    mod_s = jnp.pad(mods[:, BATCH:BATCH + DEC_BATCH], ((0, 0), (0, SAMPLE_ROWS - DEC_BATCH), (0, 0)))
    mod_s = mod_s.reshape(DEPTH, 1, SAMPLE_ROWS, 6 * D)

    tab_p = _rope_tables(np.arange(SEQ))
    tab_s = _rope_tables(np.full((SAMPLE_ROWS,), PAST))
    w_in_a_pad = jnp.pad(w_in_a, ((0, 0), (0, 0), (0, A_PROJ_PAD - A_PROJ)))
    w_router_pad = jnp.pad(w_router, ((0, 0), (0, 0), (0, LANES - N_EXP)))
    kvw = B_KV * B_DH

    ka_p, va_p, ia_p, ka_s, va_s, ia_s, kb_p, vb_p, kb_s, vb_s = ([] for _ in range(10))
    for i in range(DEPTH):
        j = i // 2
        rp = lambda tm: _Rows(tm, NP // tm, 0, mod_p[i], False)
        rs = _Rows(SAMPLE_ROWS, 1, NP // SAMPLE_ROWS, mod_s[i], True)
        if i % 2 == 0:
            pa = _proj(x, rp(1024), g_attn[i], 0, w_in_a_pad[j], tab_p, A_TILES, 512, "proj_a_prompt")
            sa = _proj(x, rs, g_attn[i], 0, w_in_a_pad[j], tab_s, A_TILES, 512, "proj_a_sample")
            ka_p.append(pa[:, 2048:2560].reshape(BATCH, SEQ, A_KV, A_DH))
            va_p.append(pa[:, 2560:3072].reshape(BATCH, SEQ, A_KV, A_DH))
            ia_p.append(pa[:, 4096:4160].reshape(BATCH, SEQ, IDX_DIM))
            sa = sa[:DEC_BATCH]
            k_new = sa[:, 2048:2560].reshape(DEC_BATCH, A_KV, A_DH)
            v_new = sa[:, 2560:3072].reshape(DEC_BATCH, A_KV, A_DH)
            ki_new = sa[:, 4096:4160].reshape(DEC_BATCH, 1, IDX_DIM)
            ka_s.append(k_new.reshape(DEC_BATCH, 1, A_KV, A_DH))
            va_s.append(v_new.reshape(DEC_BATCH, 1, A_KV, A_DH))
            ia_s.append(ki_new)
            o_p = _dsa_prompt(pa)
            scores = _dsa_score(page_table, sa[:, 3072:4096].reshape(DEC_BATCH, IDX_HEADS, IDX_DIM),
                                sa[:, 4160:4176].reshape(DEC_BATCH, IDX_HEADS, 1), ki_new, cache_kidx_a, j)
            sel = _dsa_select(scores)
            o_s = _dsa_attend(sel, page_table, sa[:, :2048].reshape(DEC_BATCH, A_HEADS, A_DH), k_new, v_new,
                              cache_k_a, cache_v_a, j)
            w_out = w_out_a[j]
        else:
            pb = _proj(x, rp(1024), g_attn[i], 0, w_in_b[j], tab_p, B_TILES, 256, "proj_b_prompt")
            sb = _proj(x, rs, g_attn[i], 0, w_in_b[j], tab_s, B_TILES, 256, "proj_b_sample")
            kb_p.append(pb[:, D:D + kvw].reshape(BATCH, SEQ, B_KV, B_DH)[:, SEQ - WINDOW:])
            vb_p.append(pb[:, D + kvw:].reshape(BATCH, SEQ, B_KV, B_DH)[:, SEQ - WINDOW:])
            sb = sb[:DEC_BATCH]
            o_p = _swa_prompt(pb, sinks_b[j])
            o_s, k_st, v_st = _swa_sample(sb[:, :D].reshape(DEC_BATCH, B_HEADS, B_DH),
                                          sb[:, D:D + kvw].reshape(DEC_BATCH, 1, kvw),
                                          sb[:, D + kvw:].reshape(DEC_BATCH, 1, kvw),
                                          state_k_b[j].reshape(DEC_BATCH, WINDOW, kvw),
                                          state_v_b[j].reshape(DEC_BATCH, WINDOW, kvw), sinks_b[j])
            kb_s.append(k_st.reshape(DEC_BATCH, WINDOW, B_KV, B_DH))
            vb_s.append(v_st.reshape(DEC_BATCH, WINDOW, B_KV, B_DH))
            w_out = w_out_b[j]
        o_s = jnp.pad(o_s.reshape(DEC_BATCH, D), ((0, SAMPLE_ROWS - DEC_BATCH), (0, 0))).astype(BF16)
        x = _out_proj(x, o_p, rp(1024), w_out, "out_proj_prompt")
        x = _out_proj(x, o_s, rs, w_out, "out_proj_sample")

        if i % 2 == 0:
            (h,) = _ffn_pre(x, rp(512), g_ffn[i], None, None)
            (h,) = _ffn_pre(x, rs, g_ffn[i], None, (h,))
            te = jnp.zeros((DENSE_TILES,), I32)
            nrows = jnp.asarray(np.array([FFN_TM] * (NP // FFN_TM) + [SAMPLE_ROWS], np.int32))
            y = _ffn(h, te, nrows, w_gate_d[j][None], w_up_d[j][None], w_down_d[j][None], DENSE_TILES)
            x = _dense_residual(x, y, rp(512))
            x = _dense_residual(x, y, rs)
        else:
            h, route = _ffn_pre(x, rp(512), g_ffn[i], w_router_pad[j], None)
            h, route = _ffn_pre(x, rs, g_ffn[i], w_router_pad[j], (h, route))
            plan = _moe_plan(route)
            xs = _moe_gather(h, plan)
            y = _ffn(xs, plan["te"], plan["nrows"], w_gate_e[j], w_up_e[j], w_down_e[j], MOE_TILES)
            x = _moe_combine(x, y, route, plan, rp(256))
            x = _moe_combine(x, y, route, plan, rs)

    y = _final_norm(x, g_final)
    return (y[:NP].reshape(BATCH, SEQ, D), y[NP:N_TOK].reshape(DEC_BATCH, 1, D),
            jnp.stack(ka_p), jnp.stack(va_p), jnp.stack(ia_p),
            jnp.stack(ka_s), jnp.stack(va_s), jnp.stack(ia_s),
            jnp.stack(kb_p), jnp.stack(vb_p), jnp.stack(kb_s), jnp.stack(vb_s))
```

```python
import functools
from typing import NamedTuple

import numpy as np
import jax
import jax.numpy as jnp
from jax import lax
from jax.experimental import pallas as pl
from jax.experimental.pallas import tpu as pltpu

F32, BF16, I32 = jnp.float32, jnp.bfloat16, jnp.int32

D = 2048
BATCH, SEQ = 4, 2048
DEPTH = 4
DEC_BATCH = 32
PAST = 16384
PAGE = 128
N_PAGES = PAST // PAGE
A_HEADS, A_KV, A_DH = 16, 4, 128
IDX_HEADS, IDX_DIM = 16, 64
TOPK = 256
B_HEADS, B_KV, B_DH = 32, 4, 64
WINDOW = 128
FFN = 7168
N_EXP = 8
ROPE_THETA = 500000.0
EPS = 1e-6
A_PROJ = 4176
A_PROJ_PAD = 4608
B_PROJ = 2560

NP = BATCH * SEQ
SAMPLE_ROWS = 256
RP = NP + SAMPLE_ROWS
N_TOK = NP + DEC_BATCH

VMEM_LIMIT = 56 * 1024 * 1024
LANES = 128

NEG_INF = float("-inf")
INT_MIN = -2147483648
KEY_NEG_INF = -2139095041

FFN_TM = 1024
FFN_CH = 256
FFN_TF = 256
N_CH = FFN_TM // FFN_CH
MOE_TILES = (2 * N_TOK) // FFN_TM + N_EXP
DENSE_TILES = NP // FFN_TM + 1

SCORE_N = (N_PAGES + 1) * PAGE


def _cp(*sem):
    return pltpu.CompilerParams(dimension_semantics=sem, vmem_limit_bytes=VMEM_LIMIT)


class _Rows(NamedTuple):
    tm: int
    n: int
    off: int
    mod: jax.Array
    per_row: bool


def _mod_spec(rows, chunk, width=D, col=None):
    per_blk = width if col is not None else D
    nblk = D // per_blk

    def idx(i, *rest):
        j = rest[0] if col is not None else 0
        c = chunk * nblk + j
        if rows.per_row:
            return (0, 0, c)
        return ((i * rows.tm) // SEQ, 0, c)

    shape = (1, rows.tm if rows.per_row else 1, per_blk)
    return pl.BlockSpec(shape, idx)


def _adaln_kernel(c_ref, w_ref, b_ref, o_ref):
    c = c_ref[...]
    a = (c / (1.0 + jnp.exp(-c))).astype(BF16)
    o_ref[0] = jnp.dot(a, w_ref[0].astype(BF16), preferred_element_type=F32) + b_ref[0]


def _adaln(c_all, w_ada, b_ada):
    tn = 1024
    rows = c_all.shape[0]
    return pl.pallas_call(
        _adaln_kernel,
        grid=(DEPTH, 6 * D // tn),
        in_specs=[pl.BlockSpec((rows, D), lambda l, j: (0, 0)),
                  pl.BlockSpec((1, D, tn), lambda l, j: (l, 0, j)),
                  pl.BlockSpec((1, 1, tn), lambda l, j: (l, 0, j))],
        out_specs=pl.BlockSpec((1, rows, tn), lambda l, j: (l, 0, j)),
        out_shape=jax.ShapeDtypeStruct((DEPTH, rows, 6 * D), F32),
        compiler_params=_cp("arbitrary", "arbitrary"),
        name="adaln",
    )(c_all, w_ada, b_ada.reshape(DEPTH, 1, 6 * D))


T_ROPE128, T_PLAIN, T_ROPE64, T_KIWI = 0, 1, 2, 3


def _rope_tables(pos):
    pos = jnp.asarray(pos, F32)[:, None]
    col = np.arange(LANES)

    def kind(head_dim, ncols):
        rot = head_dim // 4
        half = rot // 2
        inv = ROPE_THETA ** (-jnp.arange(half, dtype=F32) * 2.0 / rot)
        ang = pos * inv[None, :]
        cos, sin = jnp.cos(ang), jnp.sin(ang)
        c = col % head_dim
        lo = (c < half) & (col < ncols)
        hi = (c >= half) & (c < rot) & (col < ncols)
        f = np.where(lo, c, np.where(hi, c - half, 0))
        one = (col < ncols).astype(np.float32)
        cc = jnp.where(lo | hi, cos[:, f], one[None, :])
        aa = jnp.where(lo, -sin[:, f], 0.0)
        bb = jnp.where(hi, sin[:, f], 0.0)
        return cc, aa, bb

    k128 = kind(A_DH, LANES)
    k64 = kind(IDX_DIM, LANES)
    plain = (jnp.ones_like(k128[0]), jnp.zeros_like(k128[0]), jnp.zeros_like(k128[0]))
    kw = list(kind(IDX_DIM, IDX_DIM))
    wi_cols = (col >= IDX_DIM) & (col < IDX_DIM + IDX_HEADS)
    kw[0] = jnp.where(wi_cols[None, :], IDX_HEADS ** -0.5 * IDX_DIM ** -0.5, kw[0])
    kinds = [k128, plain, k64, tuple(kw)]
    return tuple(jnp.stack([k[t] for k in kinds]) for t in range(3))


def _modulated(x, g, shift, scale):
    ms = jnp.mean(x * x, axis=-1, keepdims=True)
    return (x * lax.rsqrt(ms + EPS) * g) * (1.0 + scale) + shift


def _proj_kernel(tt_ref, x_ref, g_ref, sh_ref, sc_ref, w_ref, c_ref, a_ref, b_ref, o_ref, h_ref, *, tn):
    j = pl.program_id(1)

    @pl.when(j == 0)
    def _():
        h_ref[...] = _modulated(x_ref[...], g_ref[...], sh_ref[0], sc_ref[0]).astype(BF16)

    y = jnp.dot(h_ref[...], w_ref[0].astype(BF16), preferred_element_type=F32)
    t = tt_ref[j]
    reps = tn // LANES

    def tab(r):
        return jnp.concatenate([r[0]] * reps, axis=1)

    @pl.when(t == T_PLAIN)
    def _():
        o_ref[...] = y

    def rope(half):
        o_ref[...] = (y * tab(c_ref) + pltpu.roll(y, tn - half, 1) * tab(a_ref)
                      + pltpu.roll(y, half, 1) * tab(b_ref))

    @pl.when(t == T_ROPE128)
    def _():
        rope(A_DH // 8)

    @pl.when(t >= T_ROPE64)
    def _():
        rope(IDX_DIM // 8)


def _proj(x, rows, g, shift_chunk, w, layer, tables, tile_types, tn, name):
    tm = rows.tm
    nj = w.shape[2] // tn
    tt = jnp.asarray(np.asarray(tile_types, np.int32))
    pos_blocks = tables[0].shape[1] // tm
    tab_spec = pl.BlockSpec((1, tm, LANES), lambda i, j, tt: (tt[j], i % pos_blocks, 0))
    grid_spec = pltpu.PrefetchScalarGridSpec(
        num_scalar_prefetch=1,
        grid=(rows.n, nj),
        in_specs=[pl.BlockSpec((tm, D), lambda i, j, tt: (i + rows.off, 0)),
                  pl.BlockSpec((1, D), lambda i, j, tt: (0, 0)),
                  _mod_spec(rows, shift_chunk),
                  _mod_spec(rows, shift_chunk + 1),
                  pl.BlockSpec((1, D, tn), lambda i, j, tt: (layer, 0, j)),
                  tab_spec, tab_spec, tab_spec],
        out_specs=pl.BlockSpec((tm, tn), lambda i, j, tt: (i, j)),
        scratch_shapes=[pltpu.VMEM((tm, D), BF16)])
    return pl.pallas_call(
        functools.partial(_proj_kernel, tn=tn),
        grid_spec=grid_spec,
        out_shape=jax.ShapeDtypeStruct((rows.n * tm, nj * tn), F32),
        compiler_params=_cp("arbitrary", "arbitrary"),
        name=name,
    )(tt, x, g.reshape(1, D), rows.mod, rows.mod, w, *tables)


def _ordered_key(x):
    b = pltpu.bitcast(x, I32)
    return b ^ ((b >> 31) & 0x7FFFFFFF)


def _select_topk(score, valid, pos, k, idx_bits, j_ref):
    rows = score.shape[0]
    key = jnp.where(valid, _ordered_key(score), KEY_NEG_INF)

    def count(m):
        return jnp.sum(jnp.where(m, 1.0, 0.0), axis=1, keepdims=True)

    def value_step(i, cand):
        trial = cand + jnp.left_shift(jnp.int32(1), 31 - i)
        return jnp.where(count(key >= trial) >= k, trial, cand)

    thr = lax.fori_loop(0, 32, value_step, jnp.full((rows, 1), INT_MIN, I32))
    gt = key > thr
    eq = key == thr
    need = k - count(gt)
    tied = jnp.where(count(eq) > need, jnp.where(thr > KEY_NEG_INF, 1.0, 0.0), 0.0)
    j_ref[...] = jnp.full((rows, 1), 1 << idx_bits, I32)

    @pl.when(jnp.max(tied) > 0.0)
    def _():
        def index_step(i, j):
            trial = j + jnp.left_shift(jnp.int32(1), idx_bits - 1 - i)
            below = jnp.where(eq, jnp.where(pos < trial, 1.0, 0.0), 0.0)
            return jnp.where(jnp.sum(below, axis=1, keepdims=True) < need, trial, j)

        j_ref[...] = lax.fori_loop(0, idx_bits, index_step, jnp.zeros((rows, 1), I32))

    return valid & (gt | (eq & (pos <= j_ref[...])))


def _dsa_prompt_body(nk, qb, q_ref, qi_ref, wq_ref, o_ref, kb, vb, kib, j_ref, tq):
    ki = kib[0:nk, :]
    score = jnp.zeros((tq, nk), F32)
    for h in range(IDX_HEADS):
        qh = qi_ref[:, h * IDX_DIM:(h + 1) * IDX_DIM].astype(BF16)
        s = lax.dot_general(qh, ki, (((1,), (1,)), ((), ())), preferred_element_type=F32)
        score = score + wq_ref[:, IDX_DIM + h:IDX_DIM + h + 1] * jnp.maximum(s, 0.0)

    kpos = lax.broadcasted_iota(I32, (tq, nk), 1)
    qpos = qb * tq + lax.broadcasted_iota(I32, (tq, nk), 0)
    sel = _select_topk(score, kpos <= qpos, kpos, min(TOPK, SEQ // 4), 11, j_ref)
    bias = jnp.where(sel, 0.0, NEG_INF)

    for h in range(A_HEADS):
        kv = h // (A_HEADS // A_KV)
        qh = q_ref[:, h * A_DH:(h + 1) * A_DH].astype(BF16)
        s = lax.dot_general(qh, kb[0:nk, kv * A_DH:(kv + 1) * A_DH], (((1,), (1,)), ((), ())),
                            preferred_element_type=F32) * (A_DH ** -0.5) + bias
        m = jnp.max(s, axis=1, keepdims=True)
        p = jnp.exp(s - m)
        l = jnp.sum(p, axis=1, keepdims=True)
        o = jnp.dot(p.astype(BF16), vb[0:nk, kv * A_DH:(kv + 1) * A_DH], preferred_element_type=F32) / l
        o_ref[:, h * A_DH:(h + 1) * A_DH] = o.astype(BF16)


def _dsa_prompt_kernel(q_ref, qi_ref, wq_ref, k_ref, v_ref, kw_ref, o_ref, kb, vb, kib, j_ref, *, tq, key_steps):
    qb = pl.program_id(1)

    @pl.when(qb == 0)
    def _():
        kb[...] = k_ref[...].astype(BF16)
        vb[...] = v_ref[...].astype(BF16)
        kib[...] = kw_ref[:, 0:IDX_DIM].astype(BF16)

    per = (SEQ // tq) // key_steps
    for step in range(key_steps):
        @pl.when(qb // per == step)
        def _():
            _dsa_prompt_body((step + 1) * per * tq, qb, q_ref, qi_ref, wq_ref, o_ref, kb, vb, kib, j_ref, tq)


def _dsa_prompt(pa):
    tq = 128
    nq = SEQ // tq
    kcol = 512
    return pl.pallas_call(
        functools.partial(_dsa_prompt_kernel, tq=tq, key_steps=4),
        grid=(BATCH, nq),
        in_specs=[pl.BlockSpec((tq, D), lambda b, i: (b * nq + i, 0)),
                  pl.BlockSpec((tq, 1024), lambda b, i: (b * nq + i, 3)),
                  pl.BlockSpec((tq, kcol), lambda b, i: (b * nq + i, 8)),
                  pl.BlockSpec((SEQ, kcol), lambda b, i: (b, 4)),
                  pl.BlockSpec((SEQ, kcol), lambda b, i: (b, 5)),
                  pl.BlockSpec((SEQ, kcol), lambda b, i: (b, 8))],
        out_specs=pl.BlockSpec((tq, D), lambda b, i: (b * nq + i, 0)),
        out_shape=jax.ShapeDtypeStruct((NP, D), BF16),
        scratch_shapes=[pltpu.VMEM((SEQ, kcol), BF16), pltpu.VMEM((SEQ, kcol), BF16),
                        pltpu.VMEM((SEQ, IDX_DIM), BF16), pltpu.VMEM((tq, 1), I32)],
        compiler_params=_cp("arbitrary", "arbitrary"),
        name="dsa_prompt",
    )(pa, pa, pa, pa, pa, pa)


def _dsa_score_kernel(pt_ref, qi_ref, wi_ref, kin_ref, cki_hbm, o_ref, kibuf, sem, *, layer):
    b = pl.program_id(0)

    def page_copy(j):
        return pltpu.make_async_copy(cki_hbm.at[layer, pt_ref[b, j]], kibuf.at[j], sem)

    def issue(j, c):
        page_copy(j).start()
        return c

    lax.fori_loop(0, N_PAGES, issue, 0)
    row = lax.broadcasted_iota(I32, (PAGE, IDX_DIM), 0)
    kibuf[N_PAGES] = jnp.where(row == 0, kin_ref[0], 0.0)

    def wait(j, c):
        page_copy(j).wait()
        return c

    lax.fori_loop(0, N_PAGES, wait, 0)
    kflat = kibuf[...].reshape(SCORE_N, IDX_DIM).astype(BF16)
    s = lax.dot_general(qi_ref[0].astype(BF16), kflat, (((1,), (1,)), ((), ())),
                        preferred_element_type=F32)
    o_ref[0] = jnp.sum(wi_ref[0] * jnp.maximum(s, 0.0), axis=0, keepdims=True)


def _dsa_score(page_table, qi, wi, ki_new, cache_kidx, layer):
    grid_spec = pltpu.PrefetchScalarGridSpec(
        num_scalar_prefetch=1,
        grid=(DEC_BATCH,),
        in_specs=[pl.BlockSpec((1, IDX_HEADS, IDX_DIM), lambda b, pt: (b, 0, 0)),
                  pl.BlockSpec((1, IDX_HEADS, 1), lambda b, pt: (b, 0, 0)),
                  pl.BlockSpec((1, 1, IDX_DIM), lambda b, pt: (b, 0, 0)),
                  pl.BlockSpec(memory_space=pl.ANY)],
        out_specs=pl.BlockSpec((1, 1, SCORE_N), lambda b, pt: (b, 0, 0)),
        scratch_shapes=[pltpu.VMEM((N_PAGES + 1, PAGE, IDX_DIM), F32), pltpu.SemaphoreType.DMA])
    out = pl.pallas_call(
        functools.partial(_dsa_score_kernel, layer=layer),
        grid_spec=grid_spec,
        out_shape=jax.ShapeDtypeStruct((DEC_BATCH, 1, SCORE_N), F32),
        compiler_params=_cp("arbitrary"),
        name="dsa_sample_score",
    )(page_table, qi, wi, ki_new, cache_kidx)
    return out.reshape(DEC_BATCH, SCORE_N)


def _dsa_select_kernel(s_ref, o_ref, j_ref, slot_ref):
    n_groups = SCORE_N // LANES
    pos = lax.broadcasted_iota(I32, (DEC_BATCH, SCORE_N), 1)
    sel = _select_topk(s_ref[...], pos <= PAST, pos, TOPK, 15, j_ref)
    self_ = jnp.where(sel, 1.0, 0.0).astype(BF16)

    gp = lax.broadcasted_iota(I32, (SCORE_N, 2 * LANES), 0) // LANES
    gg = lax.broadcasted_iota(I32, (SCORE_N, 2 * LANES), 1)
    before = jnp.dot(self_, jnp.where(gp < gg, 1.0, 0.0).astype(BF16), preferred_element_type=F32)
    tri = jnp.where(lax.broadcasted_iota(I32, (LANES, LANES), 0) <= lax.broadcasted_iota(I32, (LANES, LANES), 1),
                    1.0, 0.0).astype(BF16)
    for g in range(n_groups):
        sl = slice(g * LANES, (g + 1) * LANES)
        within = jnp.dot(self_[:, sl], tri, preferred_element_type=F32)
        slot = before[:, g:g + 1] + within - 1.0
        slot_ref[:, sl] = jnp.where(sel[:, sl], slot, -1.0)

    srow = lax.broadcasted_iota(I32, (TOPK, LANES), 0).astype(F32)
    lane = lax.broadcasted_iota(I32, (TOPK, LANES), 1)

    out = jnp.zeros((TOPK, LANES), F32)
    for b in range(DEC_BATCH):
        def per_group(g, acc):
            off = pl.multiple_of(g * LANES, LANES)
            sl = slot_ref[b:b + 1, pl.ds(off, LANES)]
            return acc + jnp.where(sl == srow, (off + lane).astype(F32), 0.0)

        acc = lax.fori_loop(0, n_groups, per_group, jnp.zeros((TOPK, LANES), F32))
        out = jnp.where(lane == b, jnp.sum(acc, axis=1, keepdims=True), out)
    o_ref[...] = out


def _dsa_select(scores):
    out = pl.pallas_call(
        _dsa_select_kernel,
        grid=(1,),
        in_specs=[pl.BlockSpec((DEC_BATCH, SCORE_N), lambda i: (0, 0))],
        out_specs=pl.BlockSpec((TOPK, LANES), lambda i: (0, 0)),
        out_shape=jax.ShapeDtypeStruct((TOPK, LANES), F32),
        scratch_shapes=[pltpu.VMEM((DEC_BATCH, 1), I32), pltpu.VMEM((DEC_BATCH, SCORE_N), F32)],
        compiler_params=_cp("arbitrary"),
        name="dsa_sample_select",
    )(scores)
    return out[:, :DEC_BATCH].T.astype(I32)


def _dsa_attend_kernel(sel_ref, pt_ref, q_ref, kn_hbm, vn_hbm, ck_hbm, cv_hbm, o_ref, kbuf, vbuf, sem, *, layer):
    b = pl.program_id(0)

    def slot_rows(s):
        return pl.ds(pl.multiple_of(s * A_KV, A_KV), A_KV)

    def issue(s, c):
        pos = sel_ref[b, s]

        @pl.when(pos < PAST)
        def _():
            phys = pt_ref[b, pos // PAGE]
            row = pos % PAGE
            pltpu.make_async_copy(ck_hbm.at[layer, phys, row], kbuf.at[slot_rows(s)], sem.at[0]).start()
            pltpu.make_async_copy(cv_hbm.at[layer, phys, row], vbuf.at[slot_rows(s)], sem.at[1]).start()

        @pl.when(pos >= PAST)
        def _():
            pltpu.make_async_copy(kn_hbm.at[b], kbuf.at[slot_rows(s)], sem.at[0]).start()
            pltpu.make_async_copy(vn_hbm.at[b], vbuf.at[slot_rows(s)], sem.at[1]).start()

        return c

    lax.fori_loop(0, TOPK, issue, 0)

    def wait(s, c):
        pltpu.make_async_copy(kn_hbm.at[b], kbuf.at[slot_rows(s)], sem.at[0]).wait()
        pltpu.make_async_copy(vn_hbm.at[b], vbuf.at[slot_rows(s)], sem.at[1]).wait()
        return c

    lax.fori_loop(0, TOPK, wait, 0)

    n = TOPK * A_KV
    s = lax.dot_general(q_ref[0].astype(BF16), kbuf[...].astype(BF16), (((1,), (1,)), ((), ())),
                        preferred_element_type=F32) * (A_DH ** -0.5)
    col = lax.broadcasted_iota(I32, (A_HEADS, n), 1)
    head = lax.broadcasted_iota(I32, (A_HEADS, n), 0)
    s = jnp.where(col % A_KV == head // (A_HEADS // A_KV), s, NEG_INF)
    m = jnp.max(s, axis=1, keepdims=True)
    p = jnp.exp(s - m)
    l = jnp.sum(p, axis=1, keepdims=True)
    o_ref[0] = jnp.dot(p.astype(BF16), vbuf[...].astype(BF16), preferred_element_type=F32) / l


def _dsa_attend(sel, page_table, q, k_new, v_new, cache_k, cache_v, layer):
    grid_spec = pltpu.PrefetchScalarGridSpec(
        num_scalar_prefetch=2,
        grid=(DEC_BATCH,),
        in_specs=[pl.BlockSpec((1, A_HEADS, A_DH), lambda b, s, pt: (b, 0, 0)),
                  pl.BlockSpec(memory_space=pl.ANY), pl.BlockSpec(memory_space=pl.ANY),
                  pl.BlockSpec(memory_space=pl.ANY), pl.BlockSpec(memory_space=pl.ANY)],
        out_specs=pl.BlockSpec((1, A_HEADS, A_DH), lambda b, s, pt: (b, 0, 0)),
        scratch_shapes=[pltpu.VMEM((TOPK * A_KV, A_DH), F32), pltpu.VMEM((TOPK * A_KV, A_DH), F32),
                        pltpu.SemaphoreType.DMA((2,))])
    return pl.pallas_call(
        functools.partial(_dsa_attend_kernel, layer=layer),
        grid_spec=grid_spec,
        out_shape=jax.ShapeDtypeStruct((DEC_BATCH, A_HEADS, A_DH), F32),
        compiler_params=_cp("arbitrary"),
        name="dsa_sample_attend",
    )(sel, page_table, q, k_new, v_new, cache_k, cache_v)


def _sink_softmax_pv(s, sink, v):
    m = jnp.maximum(jnp.max(s, axis=1, keepdims=True), sink)
    p = jnp.exp(s - m)
    den = jnp.sum(p, axis=1, keepdims=True) + jnp.exp(sink - m)
    return jnp.dot(p.astype(BF16), v, preferred_element_type=F32) / den


def _swa_prompt_kernel(q_ref, kp_ref, kc_ref, vp_ref, vc_ref, sink_ref, o_ref):
    qb = pl.program_id(1)
    k = jnp.concatenate([kp_ref[...], kc_ref[...]], axis=0).astype(BF16)
    v = jnp.concatenate([vp_ref[...], vc_ref[...]], axis=0).astype(BF16)
    qrow = lax.broadcasted_iota(I32, (WINDOW, 2 * WINDOW), 0)
    kcol = lax.broadcasted_iota(I32, (WINDOW, 2 * WINDOW), 1)
    first = jnp.where(qb > 0, 0, WINDOW)
    ok = (kcol > qrow) & (kcol <= qrow + WINDOW) & (kcol >= first)
    bias = jnp.where(ok, 0.0, NEG_INF)
    nkeys = 2 * WINDOW
    zeros = jnp.zeros((nkeys, B_DH), BF16)
    lane = lax.broadcasted_iota(I32, (WINDOW, 2 * B_DH), 1)
    g = B_HEADS // B_KV

    def block_diag(x):
        return jnp.concatenate([jnp.concatenate([x, zeros], axis=1), jnp.concatenate([zeros, x], axis=1)], axis=0)

    pairs = [(kv, kv * g + 2 * i) for kv in range(B_KV) for i in range(g // 2)]
    k2 = [block_diag(k[:, kv * B_DH:(kv + 1) * B_DH]) for kv in range(B_KV)]
    v2 = [block_diag(v[:, kv * B_DH:(kv + 1) * B_DH]) for kv in range(B_KV)]
    scores = [lax.dot_general(q_ref[:, h0 * B_DH:(h0 + 2) * B_DH].astype(BF16), k2[kv], (((1,), (1,)), ((), ())),
                              preferred_element_type=F32) for kv, h0 in pairs]
    probs, dens = [], []
    for (kv, h0), s2 in zip(pairs, scores):
        ps, ds = [], []
        for e in range(2):
            s = s2[:, e * nkeys:(e + 1) * nkeys] * (B_DH ** -0.5) + bias
            sink = sink_ref[h0 + e:h0 + e + 1, :]
            m = jnp.maximum(jnp.max(s, axis=1, keepdims=True), sink)
            p = jnp.exp(s - m)
            ps.append(p.astype(BF16))
            ds.append(jnp.sum(p, axis=1, keepdims=True) + jnp.exp(sink - m))
        probs.append(jnp.concatenate(ps, axis=1))
        dens.append(jnp.where(lane < B_DH, ds[0], ds[1]))
    for (kv, h0), p2, den in zip(pairs, probs, dens):
        o2 = jnp.dot(p2, v2[kv], preferred_element_type=F32) / den
        o_ref[:, h0 * B_DH:(h0 + 2) * B_DH] = o2.astype(BF16)


def _swa_prompt(pb, sinks):
    nq = SEQ // WINDOW
    kvw = B_KV * B_DH
    kblk, vblk = D // kvw, D // kvw + 1

    def prev(b, i):
        return b * nq + jnp.maximum(i - 1, 0)

    return pl.pallas_call(
        _swa_prompt_kernel,
        grid=(BATCH, nq),
        in_specs=[pl.BlockSpec((WINDOW, D), lambda b, i: (b * nq + i, 0)),
                  pl.BlockSpec((WINDOW, kvw), lambda b, i: (prev(b, i), kblk)),
                  pl.BlockSpec((WINDOW, kvw), lambda b, i: (b * nq + i, kblk)),
                  pl.BlockSpec((WINDOW, kvw), lambda b, i: (prev(b, i), vblk)),
                  pl.BlockSpec((WINDOW, kvw), lambda b, i: (b * nq + i, vblk)),
                  pl.BlockSpec((B_HEADS, 1), lambda b, i: (0, 0))],
        out_specs=pl.BlockSpec((WINDOW, D), lambda b, i: (b * nq + i, 0)),
        out_shape=jax.ShapeDtypeStruct((NP, D), BF16),
        compiler_params=_cp("arbitrary", "arbitrary"),
        name="swa_prompt",
    )(pb, pb, pb, pb, pb, sinks.reshape(B_HEADS, 1))


def _swa_sample_kernel(q_ref, kn_ref, vn_ref, kbuf_ref, vbuf_ref, sink_ref, o_ref, ko_ref, vo_ref):
    row = lax.broadcasted_iota(I32, (WINDOW, B_KV * B_DH), 0)
    k = jnp.where(row == WINDOW - 1, kn_ref[0], pltpu.roll(kbuf_ref[0], WINDOW - 1, 0))
    v = jnp.where(row == WINDOW - 1, vn_ref[0], pltpu.roll(vbuf_ref[0], WINDOW - 1, 0))
    ko_ref[0] = k
    vo_ref[0] = v
    kb, vb = k.astype(BF16), v.astype(BF16)
    g = B_HEADS // B_KV
    for kv in range(B_KV):
        qg = q_ref[0, kv * g:(kv + 1) * g, :].astype(BF16)
        s = lax.dot_general(qg, kb[:, kv * B_DH:(kv + 1) * B_DH], (((1,), (1,)), ((), ())),
                            preferred_element_type=F32) * (B_DH ** -0.5)
        o_ref[0, kv * g:(kv + 1) * g, :] = _sink_softmax_pv(s, sink_ref[kv * g:(kv + 1) * g, :],
                                                            vb[:, kv * B_DH:(kv + 1) * B_DH])


def _swa_sample(q, k_new, v_new, kbuf, vbuf, sinks):
    kvw = B_KV * B_DH
    state = jax.ShapeDtypeStruct((DEC_BATCH, WINDOW, kvw), F32)
    per_b3 = lambda shape: pl.BlockSpec((1,) + shape, lambda b: (b, 0, 0))
    return pl.pallas_call(
        _swa_sample_kernel,
        grid=(DEC_BATCH,),
        in_specs=[per_b3((B_HEADS, B_DH)), per_b3((1, kvw)), per_b3((1, kvw)),
                  per_b3((WINDOW, kvw)), per_b3((WINDOW, kvw)),
                  pl.BlockSpec((B_HEADS, 1), lambda b: (0, 0))],
        out_specs=[per_b3((B_HEADS, B_DH)), per_b3((WINDOW, kvw)), per_b3((WINDOW, kvw))],
        out_shape=[jax.ShapeDtypeStruct((DEC_BATCH, B_HEADS, B_DH), F32), state, state],
        compiler_params=_cp("arbitrary"),
        name="swa_sample",
    )(q, k_new, v_new, kbuf, vbuf, sinks.reshape(B_HEADS, 1))


def _out_proj_kernel(o_ref, w_ref, x_ref, gate_ref, xo_ref):
    y = jnp.dot(o_ref[...], w_ref[0].astype(BF16), preferred_element_type=F32)
    xo_ref[...] = x_ref[...] + gate_ref[0] * y


def _out_proj(x, o, rows, w, layer, name):
    tm, tn = rows.tm, 512
    return pl.pallas_call(
        _out_proj_kernel,
        grid=(rows.n, D // tn),
        in_specs=[pl.BlockSpec((tm, D), lambda i, j: (i, 0)),
                  pl.BlockSpec((1, D, tn), lambda i, j: (layer, 0, j)),
                  pl.BlockSpec((tm, tn), lambda i, j: (i + rows.off, j)),
                  _mod_spec(rows, 2, width=tn, col=True)],
        out_specs=pl.BlockSpec((tm, tn), lambda i, j: (i + rows.off, j)),
        out_shape=jax.ShapeDtypeStruct((RP, D), F32),
        input_output_aliases={2: 0},
        compiler_params=_cp("arbitrary", "arbitrary"),
        name=name,
    )(o, w, x, rows.mod)


ROW_TM = SAMPLE_ROWS
N_PROMPT_BLOCKS = NP // ROW_TM


def _both_mod_specs(chunk):
    def prompt_idx(i, *_):
        return (jnp.minimum(i, N_PROMPT_BLOCKS - 1) * ROW_TM // SEQ, 0, chunk)

    return [pl.BlockSpec((1, 1, D), prompt_idx), pl.BlockSpec((1, ROW_TM, D), lambda i, *_: (0, 0, chunk))]


def _by_row_kind(fn, *pairs):
    i = pl.program_id(0)

    @pl.when(i < N_PROMPT_BLOCKS)
    def _():
        fn(*[p[0] for p, _ in pairs])

    @pl.when(i >= N_PROMPT_BLOCKS)
    def _():
        fn(*[s[0] for _, s in pairs])


def _ffn_pre_kernel(x_ref, g_ref, shp_ref, shs_ref, scp_ref, scs_ref, wr_ref, *out_refs, moe):
    _by_row_kind(functools.partial(_ffn_pre_rows, x_ref, g_ref, wr_ref, out_refs, moe),
                 (shp_ref, shs_ref), (scp_ref, scs_ref))


def _ffn_pre_rows(x_ref, g_ref, wr_ref, out_refs, moe, shift, scale):
    h = _modulated(x_ref[...], g_ref[...], shift, scale)
    out_refs[0][...] = h
    if not moe:
        return
    logits = jnp.dot(h, wr_ref[0], preferred_element_type=F32, precision=lax.Precision.HIGHEST)
    lane = lax.broadcasted_iota(I32, logits.shape, 1).astype(F32)
    l1 = jnp.where(lane < N_EXP, logits, NEG_INF)
    m1 = jnp.max(l1, axis=1, keepdims=True)
    i1 = jnp.min(jnp.where(l1 == m1, lane, float(LANES)), axis=1, keepdims=True)
    l2 = jnp.where(lane == i1, NEG_INF, l1)
    m2 = jnp.max(l2, axis=1, keepdims=True)
    i2 = jnp.min(jnp.where(l2 == m2, lane, float(LANES)), axis=1, keepdims=True)
    e2 = jnp.exp(m2 - m1)
    g1 = 1.0 / (1.0 + e2)
    g2 = e2 / (1.0 + e2)
    out_refs[1][...] = jnp.where(lane == 0, i1, jnp.where(lane == 1, i2, jnp.where(
        lane == 2, g1, jnp.where(lane == 3, g2, 0.0))))


def _ffn_pre(x, g, mod_p, mod_s, wr, layer):
    moe = wr is not None
    row_blk = lambda w: pl.BlockSpec((ROW_TM, w), lambda i: (i, 0))
    shift, scale = _both_mod_specs(3), _both_mod_specs(4)
    in_specs = [row_blk(D), pl.BlockSpec((1, D), lambda i: (0, 0)), *shift, *scale,
                pl.BlockSpec((1, D, LANES), lambda i: (layer if moe else 0, 0, 0))]
    out_specs = [row_blk(D)] + ([row_blk(LANES)] if moe else [])
    out_shape = [jax.ShapeDtypeStruct((RP, D), F32)] + ([jax.ShapeDtypeStruct((RP, LANES), F32)] if moe else [])
    return pl.pallas_call(
        functools.partial(_ffn_pre_kernel, moe=moe), grid=(RP // ROW_TM,), in_specs=in_specs,
        out_specs=out_specs, out_shape=out_shape, compiler_params=_cp("arbitrary"), name="ffn_pre",
    )(x, g.reshape(1, D), mod_p, mod_s, mod_p, mod_s, wr if moe else jnp.zeros((1, D, LANES), F32))


def _moe_plan(route):
    e = route[:N_TOK, :2].astype(I32).reshape(-1)
    oh = (e[:, None] == jnp.arange(N_EXP, dtype=I32)[None, :]).astype(I32)
    csum = jnp.cumsum(oh, axis=0)
    rank = jnp.take_along_axis(csum - oh, e[:, None], axis=1)[:, 0]
    counts = csum[-1]
    ntile = (counts + FFN_TM - 1) // FFN_TM
    tend = jnp.cumsum(ntile)
    tstart = tend - ntile
    pos = tstart[e] * FFN_TM + rank
    src = jnp.zeros((MOE_TILES * FFN_TM,), I32).at[pos].set(jnp.arange(2 * N_TOK, dtype=I32) // 2)
    t = jnp.arange(MOE_TILES, dtype=I32)
    te = jnp.minimum(jnp.sum((t[:, None] >= tend[None, :]).astype(I32), axis=1), N_EXP - 1)
    total = tend[-1]
    nrows = jnp.where(t < total, jnp.clip(counts[te] - (t - tstart[te]) * FFN_TM, 0, FFN_TM), 0)
    te = jnp.where(t < total, te, te[total - 1])
    p = pos.reshape(N_TOK, 2)
    p = jnp.concatenate([p, jnp.broadcast_to(p[:1], (RP - N_TOK, 2))], axis=0)
    return dict(src=src, te=te, nrows=nrows.astype(I32), p1=p[:, 0], p2=p[:, 1])


N_F = FFN // FFN_TF
GATHER_PER_STEP = -(-FFN_TM // N_F)
GATHER_COPIES = GATHER_PER_STEP * N_F


def _ffn_kernel(te_ref, nr_ref, src_ref, h_hbm, wg_ref, wu_ref, wd_ref, o_ref, xf, xb, sem, *, n_tiles):
    t = pl.program_id(0)
    f = pl.program_id(1)
    n_chunks = (nr_ref[t] + FFN_CH - 1) // FFN_CH

    def row_copy(tile, k):
        r = jnp.minimum(k, FFN_TM - 1)
        return pltpu.make_async_copy(h_hbm.at[pl.ds(src_ref[tile * FFN_TM + r], 1)], xf.at[pl.ds(k, 1)], sem)

    def wait_tile():
        def wait(k, c):
            row_copy(0, k).wait()
            return c

        lax.fori_loop(0, GATHER_COPIES, wait, 0)

    @pl.when((t == 0) & (f == 0))
    def _():
        def issue(k, c):
            row_copy(0, k).start()
            return c

        lax.fori_loop(0, GATHER_COPIES, issue, 0)

    @pl.when(f == 0)
    def _():
        wait_tile()
        xb[...] = xf[0:FFN_TM, :].astype(BF16)
        o_ref[...] = jnp.zeros_like(o_ref)

    nxt = jnp.minimum(t + 1, n_tiles - 1)

    def fetch_next():
        for r in range(GATHER_PER_STEP):
            row_copy(nxt, f * GATHER_PER_STEP + r).start()

    for k in range(1, N_CH + 1):
        @pl.when(n_chunks == k)
        def _():
            fetch_next()
            rows = k * FFN_CH
            xc = xb[0:rows, :]
            gate = jnp.dot(xc, wg_ref[0, 0].astype(BF16), preferred_element_type=F32)
            up = jnp.dot(xc, wu_ref[0, 0].astype(BF16), preferred_element_type=F32)
            act = ((gate / (1.0 + jnp.exp(-gate))) * up).astype(BF16)
            o_ref[0:rows, :] += jnp.dot(act, wd_ref[0, 0].astype(BF16), preferred_element_type=F32)

    @pl.when(n_chunks == 0)
    def _():
        fetch_next()

    @pl.when((t == n_tiles - 1) & (f == N_F - 1))
    def _():
        wait_tile()


def _ffn(h, src, te, nrows, wg, wu, wd, layer, n_tiles):
    def fcol(t, f, nr):
        return jnp.where(nr[t] > 0, f, N_F - 1)

    grid_spec = pltpu.PrefetchScalarGridSpec(
        num_scalar_prefetch=3,
        grid=(n_tiles, N_F),
        in_specs=[pl.BlockSpec(memory_space=pl.ANY),
                  pl.BlockSpec((1, 1, D, FFN_TF), lambda t, f, te, nr, src: (layer, te[t], 0, fcol(t, f, nr))),
                  pl.BlockSpec((1, 1, D, FFN_TF), lambda t, f, te, nr, src: (layer, te[t], 0, fcol(t, f, nr))),
                  pl.BlockSpec((1, 1, FFN_TF, D), lambda t, f, te, nr, src: (layer, te[t], fcol(t, f, nr), 0))],
        out_specs=pl.BlockSpec((FFN_TM, D), lambda t, f, te, nr, src: (t, 0)),
        scratch_shapes=[pltpu.VMEM((-(-GATHER_COPIES // 8) * 8, D), F32), pltpu.VMEM((FFN_TM, D), BF16),
                        pltpu.SemaphoreType.DMA])
    return pl.pallas_call(
        functools.partial(_ffn_kernel, n_tiles=n_tiles), grid_spec=grid_spec,
        out_shape=jax.ShapeDtypeStruct((n_tiles * FFN_TM, D), F32),
        compiler_params=_cp("arbitrary", "arbitrary"), name="ffn",
    )(te, nrows, src, h, wg, wu, wd)


def _dense_residual_kernel(x_ref, y_ref, gp_ref, gs_ref, o_ref):
    def rows(gate):
        o_ref[...] = x_ref[...] + gate * y_ref[...]

    _by_row_kind(rows, (gp_ref, gs_ref))


def _dense_residual(x, y, mod_p, mod_s):
    blk = pl.BlockSpec((ROW_TM, D), lambda i: (i, 0))
    return pl.pallas_call(
        _dense_residual_kernel, grid=(RP // ROW_TM,),
        in_specs=[blk, blk, *_both_mod_specs(5)], out_specs=blk,
        out_shape=jax.ShapeDtypeStruct((RP, D), F32), input_output_aliases={0: 0},
        compiler_params=_cp("arbitrary"), name="dense_residual",
    )(x, y, mod_p, mod_s)


def _moe_combine_kernel(p1_ref, p2_ref, x_ref, r_ref, gp_ref, gs_ref, y_hbm, o_ref, ybuf, sem):
    base = pl.program_id(0) * ROW_TM

    def row_copy(r, k, p):
        return pltpu.make_async_copy(y_hbm.at[pl.ds(p, 1)], ybuf.at[k, pl.ds(r, 1)], sem.at[k])

    def issue(r, c):
        row_copy(r, 0, p1_ref[base + r]).start()
        row_copy(r, 1, p2_ref[base + r]).start()
        return c

    lax.fori_loop(0, ROW_TM, issue, 0)

    def wait(r, c):
        row_copy(r, 0, 0).wait()
        row_copy(r, 1, 0).wait()
        return c

    lax.fori_loop(0, ROW_TM, wait, 0)

    def rows(gate):
        mix = r_ref[:, 2:3] * ybuf[0] + r_ref[:, 3:4] * ybuf[1]
        o_ref[...] = x_ref[...] + gate * mix

    _by_row_kind(rows, (gp_ref, gs_ref))


def _moe_combine(x, y, route, plan, mod_p, mod_s):
    blk = lambda w: pl.BlockSpec((ROW_TM, w), lambda i, p1, p2: (i, 0))
    grid_spec = pltpu.PrefetchScalarGridSpec(
        num_scalar_prefetch=2,
        grid=(RP // ROW_TM,),
        in_specs=[blk(D), blk(LANES), *_both_mod_specs(5), pl.BlockSpec(memory_space=pl.ANY)],
        out_specs=blk(D),
        scratch_shapes=[pltpu.VMEM((2, ROW_TM, D), F32), pltpu.SemaphoreType.DMA((2,))])
    return pl.pallas_call(
        _moe_combine_kernel, grid_spec=grid_spec,
        out_shape=jax.ShapeDtypeStruct((RP, D), F32), input_output_aliases={2: 0},
        compiler_params=_cp("arbitrary"), name="moe_combine",
    )(plan["p1"], plan["p2"], x, route, mod_p, mod_s, y)


def _final_norm_kernel(x_ref, g_ref, o_ref):
    x = x_ref[...]
    ms = jnp.mean(x * x, axis=-1, keepdims=True)
    o_ref[...] = x * lax.rsqrt(ms + EPS) * g_ref[...]


def _final_norm(x, g):
    tm = 256
    return pl.pallas_call(
        _final_norm_kernel, grid=(RP // tm,),
        in_specs=[pl.BlockSpec((tm, D), lambda i: (i, 0)), pl.BlockSpec((1, D), lambda i: (0, 0))],
        out_specs=pl.BlockSpec((tm, D), lambda i: (i, 0)),
        out_shape=jax.ShapeDtypeStruct((RP, D), F32),
        compiler_params=_cp("arbitrary"), name="final_norm",
    )(x, g.reshape(1, D))


A_TILES = (T_ROPE128,) * 5 + (T_PLAIN,) + (T_ROPE64,) * 2 + (T_KIWI,)
B_TILES = (T_ROPE64,) * 9 + (T_PLAIN,)


def kernel(x_prompt, x_sample, cache_k_a, cache_v_a, cache_kidx_a, state_k_b, state_v_b, page_table,
           c_prompt, c_sample, w_ada, b_ada, g_attn, g_ffn, w_in_a, w_out_a, w_in_b, w_out_b, sinks_b,
           w_gate_d, w_up_d, w_down_d, w_router, w_gate_e, w_up_e, w_down_e, g_final):
    x = jnp.concatenate([x_prompt.reshape(NP, D), x_sample.reshape(DEC_BATCH, D),
                         jnp.zeros((SAMPLE_ROWS - DEC_BATCH, D), F32)], axis=0)
    c_all = jnp.concatenate([c_prompt, c_sample, jnp.zeros((4, D), F32)], axis=0)
    mods = _adaln(c_all, w_ada, b_ada)
    mod_p = mods[:, :BATCH].reshape(DEPTH, BATCH, 1, 6 * D)
    mod_s = jnp.pad(mods[:, BATCH:BATCH + DEC_BATCH], ((0, 0), (0, SAMPLE_ROWS - DEC_BATCH), (0, 0)))
    mod_s = mod_s.reshape(DEPTH, 1, SAMPLE_ROWS, 6 * D)

    tab_p = _rope_tables(np.arange(SEQ))
    tab_s = _rope_tables(np.full((SAMPLE_ROWS,), PAST))
    w_in_a_pad = jnp.pad(w_in_a, ((0, 0), (0, 0), (0, A_PROJ_PAD - A_PROJ)))
    w_router_pad = jnp.pad(w_router, ((0, 0), (0, 0), (0, LANES - N_EXP)))
    kvw = B_KV * B_DH
    dense_src = jnp.asarray(np.minimum(np.arange(DENSE_TILES * FFN_TM), RP - 1).astype(np.int32))
    dense_te = jnp.zeros((DENSE_TILES,), I32)
    dense_rows = jnp.asarray(np.array([FFN_TM] * (NP // FFN_TM) + [SAMPLE_ROWS], np.int32))

    ka_p, va_p, ia_p, ka_s, va_s, ia_s, kb_p, vb_p, kb_s, vb_s = ([] for _ in range(10))
    for i in range(DEPTH):
        j = i // 2
        rp = lambda tm: _Rows(tm, NP // tm, 0, mod_p[i], False)
        rs = _Rows(SAMPLE_ROWS, 1, NP // SAMPLE_ROWS, mod_s[i], True)
        if i % 2 == 0:
            pa = _proj(x, rp(1024), g_attn[i], 0, w_in_a_pad, j, tab_p, A_TILES, 512, "proj_a_prompt")
            sa = _proj(x, rs, g_attn[i], 0, w_in_a_pad, j, tab_s, A_TILES, 512, "proj_a_sample")
            ka_p.append(pa[:, 2048:2560].reshape(BATCH, SEQ, A_KV, A_DH))
            va_p.append(pa[:, 2560:3072].reshape(BATCH, SEQ, A_KV, A_DH))
            ia_p.append(pa[:, 4096:4160].reshape(BATCH, SEQ, IDX_DIM))
            sa = sa[:DEC_BATCH]
            k_new = sa[:, 2048:2560].reshape(DEC_BATCH, A_KV, A_DH)
            v_new = sa[:, 2560:3072].reshape(DEC_BATCH, A_KV, A_DH)
            ki_new = sa[:, 4096:4160].reshape(DEC_BATCH, 1, IDX_DIM)
            ka_s.append(k_new.reshape(DEC_BATCH, 1, A_KV, A_DH))
            va_s.append(v_new.reshape(DEC_BATCH, 1, A_KV, A_DH))
            ia_s.append(ki_new)
            o_p = _dsa_prompt(pa)
            scores = _dsa_score(page_table, sa[:, 3072:4096].reshape(DEC_BATCH, IDX_HEADS, IDX_DIM),
                                sa[:, 4160:4176].reshape(DEC_BATCH, IDX_HEADS, 1), ki_new, cache_kidx_a, j)
            sel = _dsa_select(scores)
            o_s = _dsa_attend(sel, page_table, sa[:, :2048].reshape(DEC_BATCH, A_HEADS, A_DH), k_new, v_new,
                              cache_k_a, cache_v_a, j)
            w_out = w_out_a
        else:
            pb = _proj(x, rp(1024), g_attn[i], 0, w_in_b, j, tab_p, B_TILES, 256, "proj_b_prompt")
            sb = _proj(x, rs, g_attn[i], 0, w_in_b, j, tab_s, B_TILES, 256, "proj_b_sample")
            kb_p.append(pb[:, D:D + kvw].reshape(BATCH, SEQ, B_KV, B_DH)[:, SEQ - WINDOW:])
            vb_p.append(pb[:, D + kvw:].reshape(BATCH, SEQ, B_KV, B_DH)[:, SEQ - WINDOW:])
            sb = sb[:DEC_BATCH]
            o_p = _swa_prompt(pb, sinks_b[j])
            o_s, k_st, v_st = _swa_sample(sb[:, :D].reshape(DEC_BATCH, B_HEADS, B_DH),
                                          sb[:, D:D + kvw].reshape(DEC_BATCH, 1, kvw),
                                          sb[:, D + kvw:].reshape(DEC_BATCH, 1, kvw),
                                          state_k_b[j].reshape(DEC_BATCH, WINDOW, kvw),
                                          state_v_b[j].reshape(DEC_BATCH, WINDOW, kvw), sinks_b[j])
            kb_s.append(k_st.reshape(DEC_BATCH, WINDOW, B_KV, B_DH))
            vb_s.append(v_st.reshape(DEC_BATCH, WINDOW, B_KV, B_DH))
            w_out = w_out_b
        o_s = jnp.pad(o_s.reshape(DEC_BATCH, D), ((0, SAMPLE_ROWS - DEC_BATCH), (0, 0))).astype(BF16)
        x = _out_proj(x, o_p, rp(1024), w_out, j, "out_proj_prompt")
        x = _out_proj(x, o_s, rs, w_out, j, "out_proj_sample")

        if i % 2 == 0:
            (h,) = _ffn_pre(x, g_ffn[i], mod_p[i], mod_s[i], None, j)
            y = _ffn(h, dense_src, dense_te, dense_rows, w_gate_d[:, None], w_up_d[:, None], w_down_d[:, None],
                     j, DENSE_TILES)
            x = _dense_residual(x, y, mod_p[i], mod_s[i])
        else:
            h, route = _ffn_pre(x, g_ffn[i], mod_p[i], mod_s[i], w_router_pad, j)
            plan = _moe_plan(route)
            y = _ffn(h, plan["src"], plan["te"], plan["nrows"], w_gate_e, w_up_e, w_down_e, j, MOE_TILES)
            x = _moe_combine(x, y, route, plan, mod_p[i], mod_s[i])

    y = _final_norm(x, g_final)
    return (y[:NP].reshape(BATCH, SEQ, D), y[NP:N_TOK].reshape(DEC_BATCH, 1, D),
            jnp.stack(ka_p), jnp.stack(va_p), jnp.stack(ia_p),
            jnp.stack(ka_s), jnp.stack(va_s), jnp.stack(ia_s),
            jnp.stack(kb_p), jnp.stack(vb_p), jnp.stack(kb_s), jnp.stack(vb_s))
```

```python
import functools
from typing import NamedTuple

import numpy as np
import jax
import jax.numpy as jnp
from jax import lax
from jax.experimental import pallas as pl
from jax.experimental.pallas import tpu as pltpu

F32, BF16, I32 = jnp.float32, jnp.bfloat16, jnp.int32

D = 2048
BATCH, SEQ = 4, 2048
DEPTH = 4
DEC_BATCH = 32
PAST = 16384
PAGE = 128
N_PAGES = PAST // PAGE
A_HEADS, A_KV, A_DH = 16, 4, 128
IDX_HEADS, IDX_DIM = 16, 64
TOPK = 256
B_HEADS, B_KV, B_DH = 32, 4, 64
WINDOW = 128
FFN = 7168
N_EXP = 8
ROPE_THETA = 500000.0
EPS = 1e-6
A_PROJ = 4176
A_PROJ_PAD = 4608
B_PROJ = 2560

NP = BATCH * SEQ
SAMPLE_ROWS = 256
RP = NP + SAMPLE_ROWS
N_TOK = NP + DEC_BATCH

VMEM_LIMIT = 56 * 1024 * 1024
LANES = 128

NEG_INF = float("-inf")
INT_MIN = -2147483648
KEY_NEG_INF = -2139095041

FFN_TM = 1152
FFN_CH = 128
FFN_TF = 256
N_CH = FFN_TM // FFN_CH
MOE_TILES = (2 * N_TOK) // FFN_TM + N_EXP
DENSE_TILES = -(-RP // FFN_TM)

SCORE_N = (N_PAGES + 1) * PAGE


def _cp(*sem):
    return pltpu.CompilerParams(dimension_semantics=sem, vmem_limit_bytes=VMEM_LIMIT)


class _Rows(NamedTuple):
    tm: int
    n: int
    off: int
    mod: jax.Array
    per_row: bool


def _mod_spec(rows, chunk, width=D, col=None):
    per_blk = width if col is not None else D
    nblk = D // per_blk

    def idx(i, *rest):
        j = rest[0] if col is not None else 0
        c = chunk * nblk + j
        if rows.per_row:
            return (0, 0, c)
        return ((i * rows.tm) // SEQ, 0, c)

    shape = (1, rows.tm if rows.per_row else 1, per_blk)
    return pl.BlockSpec(shape, idx)


def _adaln_kernel(c_ref, w_ref, b_ref, o_ref):
    c = c_ref[...]
    a = (c / (1.0 + jnp.exp(-c))).astype(BF16)
    o_ref[0] = jnp.dot(a, w_ref[0].astype(BF16), preferred_element_type=F32) + b_ref[0]


def _adaln(c_all, w_ada, b_ada):
    tn = 1024
    rows = c_all.shape[0]
    return pl.pallas_call(
        _adaln_kernel,
        grid=(DEPTH, 6 * D // tn),
        in_specs=[pl.BlockSpec((rows, D), lambda l, j: (0, 0)),
                  pl.BlockSpec((1, D, tn), lambda l, j: (l, 0, j)),
                  pl.BlockSpec((1, 1, tn), lambda l, j: (l, 0, j))],
        out_specs=pl.BlockSpec((1, rows, tn), lambda l, j: (l, 0, j)),
        out_shape=jax.ShapeDtypeStruct((DEPTH, rows, 6 * D), F32),
        compiler_params=_cp("arbitrary", "arbitrary"),
        name="adaln",
    )(c_all, w_ada, b_ada.reshape(DEPTH, 1, 6 * D))


T_ROPE128, T_PLAIN, T_ROPE64, T_KIWI = 0, 1, 2, 3


def _rope_tables(pos):
    pos = jnp.asarray(pos, F32)[:, None]
    col = np.arange(LANES)

    def kind(head_dim, ncols):
        rot = head_dim // 4
        half = rot // 2
        inv = ROPE_THETA ** (-jnp.arange(half, dtype=F32) * 2.0 / rot)
        ang = pos * inv[None, :]
        cos, sin = jnp.cos(ang), jnp.sin(ang)
        c = col % head_dim
        lo = (c < half) & (col < ncols)
        hi = (c >= half) & (c < rot) & (col < ncols)
        f = np.where(lo, c, np.where(hi, c - half, 0))
        one = (col < ncols).astype(np.float32)
        cc = jnp.where(lo | hi, cos[:, f], one[None, :])
        aa = jnp.where(lo, -sin[:, f], 0.0)
        bb = jnp.where(hi, sin[:, f], 0.0)
        return cc, aa, bb

    k128 = kind(A_DH, LANES)
    k64 = kind(IDX_DIM, LANES)
    plain = (jnp.ones_like(k128[0]), jnp.zeros_like(k128[0]), jnp.zeros_like(k128[0]))
    kw = list(kind(IDX_DIM, IDX_DIM))
    wi_cols = (col >= IDX_DIM) & (col < IDX_DIM + IDX_HEADS)
    kw[0] = jnp.where(wi_cols[None, :], IDX_HEADS ** -0.5 * IDX_DIM ** -0.5, kw[0])
    kinds = [k128, plain, k64, tuple(kw)]
    return tuple(jnp.stack([k[t] for k in kinds]) for t in range(3))


def _modulated(x, g, shift, scale):
    ms = jnp.mean(x * x, axis=-1, keepdims=True)
    return (x * lax.rsqrt(ms + EPS) * g) * (1.0 + scale) + shift


def _proj_kernel(tt_ref, x_ref, g_ref, sh_ref, sc_ref, w_ref, c_ref, a_ref, b_ref, o_ref, h_ref, *, tn):
    j = pl.program_id(1)

    @pl.when(j == 0)
    def _():
        h_ref[...] = _modulated(x_ref[...], g_ref[...], sh_ref[0], sc_ref[0]).astype(BF16)

    y = jnp.dot(h_ref[...], w_ref[0].astype(BF16), preferred_element_type=F32)
    t = tt_ref[j]
    reps = tn // LANES

    def tab(r):
        return jnp.concatenate([r[0]] * reps, axis=1)

    @pl.when(t == T_PLAIN)
    def _():
        o_ref[...] = y

    def rope(half):
        o_ref[...] = (y * tab(c_ref) + pltpu.roll(y, tn - half, 1) * tab(a_ref)
                      + pltpu.roll(y, half, 1) * tab(b_ref))

    @pl.when(t == T_ROPE128)
    def _():
        rope(A_DH // 8)

    @pl.when(t >= T_ROPE64)
    def _():
        rope(IDX_DIM // 8)


def _proj(x, rows, g, shift_chunk, w, layer, tables, tile_types, tn, name):
    tm = rows.tm
    nj = w.shape[2] // tn
    tt = jnp.asarray(np.asarray(tile_types, np.int32))
    pos_blocks = tables[0].shape[1] // tm
    tab_spec = pl.BlockSpec((1, tm, LANES), lambda i, j, tt: (tt[j], i % pos_blocks, 0))
    grid_spec = pltpu.PrefetchScalarGridSpec(
        num_scalar_prefetch=1,
        grid=(rows.n, nj),
        in_specs=[pl.BlockSpec((tm, D), lambda i, j, tt: (i + rows.off, 0)),
                  pl.BlockSpec((1, D), lambda i, j, tt: (0, 0)),
                  _mod_spec(rows, shift_chunk),
                  _mod_spec(rows, shift_chunk + 1),
                  pl.BlockSpec((1, D, tn), lambda i, j, tt: (layer, 0, j)),
                  tab_spec, tab_spec, tab_spec],
        out_specs=pl.BlockSpec((tm, tn), lambda i, j, tt: (i, j)),
        scratch_shapes=[pltpu.VMEM((tm, D), BF16)])
    return pl.pallas_call(
        functools.partial(_proj_kernel, tn=tn),
        grid_spec=grid_spec,
        out_shape=jax.ShapeDtypeStruct((rows.n * tm, nj * tn), F32),
        compiler_params=_cp("arbitrary", "arbitrary"),
        name=name,
    )(tt, x, g.reshape(1, D), rows.mod, rows.mod, w, *tables)


def _ordered_key(x):
    b = pltpu.bitcast(x, I32)
    return b ^ ((b >> 31) & 0x7FFFFFFF)


def _select_topk(score, valid, pos, k, idx_bits, j_ref):
    rows = score.shape[0]
    key = jnp.where(valid, _ordered_key(score), KEY_NEG_INF)

    def count(m):
        return jnp.sum(jnp.where(m, 1.0, 0.0), axis=1, keepdims=True)

    def value_step(i, cand):
        trial = cand + jnp.left_shift(jnp.int32(1), 31 - i)
        return jnp.where(count(key >= trial) >= k, trial, cand)

    thr = lax.fori_loop(0, 32, value_step, jnp.full((rows, 1), INT_MIN, I32))
    gt = key > thr
    eq = key == thr
    need = k - count(gt)
    tied = jnp.where(count(eq) > need, jnp.where(thr > KEY_NEG_INF, 1.0, 0.0), 0.0)
    j_ref[...] = jnp.full((rows, 1), 1 << idx_bits, I32)

    @pl.when(jnp.max(tied) > 0.0)
    def _():
        def index_step(i, j):
            trial = j + jnp.left_shift(jnp.int32(1), idx_bits - 1 - i)
            below = jnp.where(eq, jnp.where(pos < trial, 1.0, 0.0), 0.0)
            return jnp.where(jnp.sum(below, axis=1, keepdims=True) < need, trial, j)

        j_ref[...] = lax.fori_loop(0, idx_bits, index_step, jnp.zeros((rows, 1), I32))

    return valid & (gt | (eq & (pos <= j_ref[...])))


def _dsa_prompt_body(nk, qb, q_ref, qi_ref, wq_ref, o_ref, kb, vb, kib, j_ref, tq):
    ki = kib[0:nk, :]
    score = jnp.zeros((tq, nk), F32)
    for h in range(IDX_HEADS):
        qh = qi_ref[:, h * IDX_DIM:(h + 1) * IDX_DIM].astype(BF16)
        s = lax.dot_general(qh, ki, (((1,), (1,)), ((), ())), preferred_element_type=F32)
        score = score + wq_ref[:, IDX_DIM + h:IDX_DIM + h + 1] * jnp.maximum(s, 0.0)

    kpos = lax.broadcasted_iota(I32, (tq, nk), 1)
    qpos = qb * tq + lax.broadcasted_iota(I32, (tq, nk), 0)
    sel = _select_topk(score, kpos <= qpos, kpos, min(TOPK, SEQ // 4), 11, j_ref)
    bias = jnp.where(sel, 0.0, NEG_INF)

    for h in range(A_HEADS):
        kv = h // (A_HEADS // A_KV)
        qh = q_ref[:, h * A_DH:(h + 1) * A_DH].astype(BF16)
        s = lax.dot_general(qh, kb[0:nk, kv * A_DH:(kv + 1) * A_DH], (((1,), (1,)), ((), ())),
                            preferred_element_type=F32) * (A_DH ** -0.5) + bias
        m = jnp.max(s, axis=1, keepdims=True)
        p = jnp.exp(s - m)
        l = jnp.sum(p, axis=1, keepdims=True)
        o = jnp.dot(p.astype(BF16), vb[0:nk, kv * A_DH:(kv + 1) * A_DH], preferred_element_type=F32) / l
        o_ref[:, h * A_DH:(h + 1) * A_DH] = o.astype(BF16)


def _dsa_prompt_kernel(q_ref, qi_ref, wq_ref, k_ref, v_ref, kw_ref, o_ref, kb, vb, kib, j_ref, *, tq, key_steps):
    qb = pl.program_id(1)

    @pl.when(qb == 0)
    def _():
        kb[...] = k_ref[...].astype(BF16)
        vb[...] = v_ref[...].astype(BF16)
        kib[...] = kw_ref[:, 0:IDX_DIM].astype(BF16)

    per = (SEQ // tq) // key_steps
    for step in range(key_steps):
        @pl.when(qb // per == step)
        def _():
            _dsa_prompt_body((step + 1) * per * tq, qb, q_ref, qi_ref, wq_ref, o_ref, kb, vb, kib, j_ref, tq)


def _dsa_prompt(pa):
    tq = 128
    nq = SEQ // tq
    kcol = 512
    return pl.pallas_call(
        functools.partial(_dsa_prompt_kernel, tq=tq, key_steps=4),
        grid=(BATCH, nq),
        in_specs=[pl.BlockSpec((tq, D), lambda b, i: (b * nq + i, 0)),
                  pl.BlockSpec((tq, 1024), lambda b, i: (b * nq + i, 3)),
                  pl.BlockSpec((tq, kcol), lambda b, i: (b * nq + i, 8)),
                  pl.BlockSpec((SEQ, kcol), lambda b, i: (b, 4)),
                  pl.BlockSpec((SEQ, kcol), lambda b, i: (b, 5)),
                  pl.BlockSpec((SEQ, kcol), lambda b, i: (b, 8))],
        out_specs=pl.BlockSpec((tq, D), lambda b, i: (b * nq + i, 0)),
        out_shape=jax.ShapeDtypeStruct((NP, D), BF16),
        scratch_shapes=[pltpu.VMEM((SEQ, kcol), BF16), pltpu.VMEM((SEQ, kcol), BF16),
                        pltpu.VMEM((SEQ, IDX_DIM), BF16), pltpu.VMEM((tq, 1), I32)],
        compiler_params=_cp("arbitrary", "arbitrary"),
        name="dsa_prompt",
    )(pa, pa, pa, pa, pa, pa)


def _dsa_score_kernel(pt_ref, qi_ref, wi_ref, kin_ref, cki_hbm, o_ref, kt, sem, *, layer):
    b = pl.program_id(0)

    def page_copy(j):
        off = pl.multiple_of(j * PAGE, PAGE)
        return pltpu.make_async_copy(cki_hbm.at[layer, pt_ref[b, j]], kt.at[:, pl.ds(off, PAGE)], sem)

    def issue(j, c):
        page_copy(j).start()
        return c

    lax.fori_loop(0, N_PAGES, issue, 0)
    lane = lax.broadcasted_iota(I32, (IDX_DIM, PAGE), 1)
    kt[:, PAST:PAST + PAGE] = jnp.where(lane == 0, kin_ref[0], 0.0)

    def wait(j, c):
        page_copy(j).wait()
        return c

    lax.fori_loop(0, N_PAGES, wait, 0)
    s = jnp.dot(qi_ref[0].astype(BF16), kt[...].astype(BF16), preferred_element_type=F32)
    o_ref[0] = jnp.sum(wi_ref[0] * jnp.maximum(s, 0.0), axis=0, keepdims=True)


def _dsa_score(page_table, qi, wi, ki_new, cache_kidx, layer):
    grid_spec = pltpu.PrefetchScalarGridSpec(
        num_scalar_prefetch=1,
        grid=(DEC_BATCH,),
        in_specs=[pl.BlockSpec((1, IDX_HEADS, IDX_DIM), lambda b, pt: (b, 0, 0)),
                  pl.BlockSpec((1, IDX_HEADS, 1), lambda b, pt: (b, 0, 0)),
                  pl.BlockSpec((1, IDX_DIM, 1), lambda b, pt: (b, 0, 0)),
                  pl.BlockSpec(memory_space=pl.ANY)],
        out_specs=pl.BlockSpec((1, 1, SCORE_N), lambda b, pt: (b, 0, 0)),
        scratch_shapes=[pltpu.VMEM((IDX_DIM, SCORE_N), F32), pltpu.SemaphoreType.DMA])
    out = pl.pallas_call(
        functools.partial(_dsa_score_kernel, layer=layer),
        grid_spec=grid_spec,
        out_shape=jax.ShapeDtypeStruct((DEC_BATCH, 1, SCORE_N), F32),
        compiler_params=_cp("arbitrary"),
        name="dsa_sample_score",
    )(page_table, qi, wi, ki_new, cache_kidx)
    return out.reshape(DEC_BATCH, SCORE_N)


def _dsa_select_kernel(s_ref, o_ref, j_ref, slot_ref):
    n_groups = SCORE_N // LANES
    pos = lax.broadcasted_iota(I32, (DEC_BATCH, SCORE_N), 1)
    sel = _select_topk(s_ref[...], pos <= PAST, pos, TOPK, 15, j_ref)
    self_ = jnp.where(sel, 1.0, 0.0).astype(BF16)

    gp = lax.broadcasted_iota(I32, (SCORE_N, 2 * LANES), 0) // LANES
    gg = lax.broadcasted_iota(I32, (SCORE_N, 2 * LANES), 1)
    before = jnp.dot(self_, jnp.where(gp < gg, 1.0, 0.0).astype(BF16), preferred_element_type=F32)
    tri = jnp.where(lax.broadcasted_iota(I32, (LANES, LANES), 0) <= lax.broadcasted_iota(I32, (LANES, LANES), 1),
                    1.0, 0.0).astype(BF16)
    for g in range(n_groups):
        sl = slice(g * LANES, (g + 1) * LANES)
        within = jnp.dot(self_[:, sl], tri, preferred_element_type=F32)
        slot = before[:, g:g + 1] + within - 1.0
        slot_ref[:, sl] = jnp.where(sel[:, sl], slot, -1.0)

    srow = lax.broadcasted_iota(I32, (TOPK, LANES), 0).astype(F32)
    lane = lax.broadcasted_iota(I32, (TOPK, LANES), 1)

    out = jnp.zeros((TOPK, LANES), F32)
    for b in range(DEC_BATCH):
        def per_group(g, acc):
            off = pl.multiple_of(g * LANES, LANES)
            sl = slot_ref[b:b + 1, pl.ds(off, LANES)]
            return acc + jnp.where(sl == srow, (off + lane).astype(F32), 0.0)

        acc = lax.fori_loop(0, n_groups, per_group, jnp.zeros((TOPK, LANES), F32))
        out = jnp.where(lane == b, jnp.sum(acc, axis=1, keepdims=True), out)
    o_ref[...] = out


def _dsa_select(scores):
    out = pl.pallas_call(
        _dsa_select_kernel,
        grid=(1,),
        in_specs=[pl.BlockSpec((DEC_BATCH, SCORE_N), lambda i: (0, 0))],
        out_specs=pl.BlockSpec((TOPK, LANES), lambda i: (0, 0)),
        out_shape=jax.ShapeDtypeStruct((TOPK, LANES), F32),
        scratch_shapes=[pltpu.VMEM((DEC_BATCH, 1), I32), pltpu.VMEM((DEC_BATCH, SCORE_N), F32)],
        compiler_params=_cp("arbitrary"),
        name="dsa_sample_select",
    )(scores)
    return out[:, :DEC_BATCH].T.astype(I32)


def _dsa_attend_kernel(sel_ref, pt_ref, q_ref, kn_hbm, vn_hbm, ck_hbm, cv_hbm, o_ref, kbuf, vbuf, sem, *, layer):
    b = pl.program_id(0)

    def slot_rows(s):
        return pl.ds(pl.multiple_of(s * A_KV, A_KV), A_KV)

    def issue(s, c):
        pos = sel_ref[b, s]

        @pl.when(pos < PAST)
        def _():
            phys = pt_ref[b, pos // PAGE]
            row = pos % PAGE
            pltpu.make_async_copy(ck_hbm.at[layer, phys, row], kbuf.at[slot_rows(s)], sem.at[0]).start()
            pltpu.make_async_copy(cv_hbm.at[layer, phys, row], vbuf.at[slot_rows(s)], sem.at[1]).start()

        @pl.when(pos >= PAST)
        def _():
            pltpu.make_async_copy(kn_hbm.at[b], kbuf.at[slot_rows(s)], sem.at[0]).start()
            pltpu.make_async_copy(vn_hbm.at[b], vbuf.at[slot_rows(s)], sem.at[1]).start()

        return c

    lax.fori_loop(0, TOPK, issue, 0)

    def wait(s, c):
        pltpu.make_async_copy(kn_hbm.at[b], kbuf.at[slot_rows(s)], sem.at[0]).wait()
        pltpu.make_async_copy(vn_hbm.at[b], vbuf.at[slot_rows(s)], sem.at[1]).wait()
        return c

    lax.fori_loop(0, TOPK, wait, 0)

    n = TOPK * A_KV
    s = lax.dot_general(q_ref[0].astype(BF16), kbuf[...].astype(BF16), (((1,), (1,)), ((), ())),
                        preferred_element_type=F32) * (A_DH ** -0.5)
    col = lax.broadcasted_iota(I32, (A_HEADS, n), 1)
    head = lax.broadcasted_iota(I32, (A_HEADS, n), 0)
    s = jnp.where(col % A_KV == head // (A_HEADS // A_KV), s, NEG_INF)
    m = jnp.max(s, axis=1, keepdims=True)
    p = jnp.exp(s - m)
    l = jnp.sum(p, axis=1, keepdims=True)
    o_ref[0] = jnp.dot(p.astype(BF16), vbuf[...].astype(BF16), preferred_element_type=F32) / l


def _dsa_attend(sel, page_table, q, k_new, v_new, cache_k, cache_v, layer):
    grid_spec = pltpu.PrefetchScalarGridSpec(
        num_scalar_prefetch=2,
        grid=(DEC_BATCH,),
        in_specs=[pl.BlockSpec((1, A_HEADS, A_DH), lambda b, s, pt: (b, 0, 0)),
                  pl.BlockSpec(memory_space=pl.ANY), pl.BlockSpec(memory_space=pl.ANY),
                  pl.BlockSpec(memory_space=pl.ANY), pl.BlockSpec(memory_space=pl.ANY)],
        out_specs=pl.BlockSpec((1, A_HEADS, A_DH), lambda b, s, pt: (b, 0, 0)),
        scratch_shapes=[pltpu.VMEM((TOPK * A_KV, A_DH), F32), pltpu.VMEM((TOPK * A_KV, A_DH), F32),
                        pltpu.SemaphoreType.DMA((2,))])
    return pl.pallas_call(
        functools.partial(_dsa_attend_kernel, layer=layer),
        grid_spec=grid_spec,
        out_shape=jax.ShapeDtypeStruct((DEC_BATCH, A_HEADS, A_DH), F32),
        compiler_params=_cp("arbitrary"),
        name="dsa_sample_attend",
    )(sel, page_table, q, k_new, v_new, cache_k, cache_v)


def _sink_softmax_pv(s, sink, v):
    m = jnp.maximum(jnp.max(s, axis=1, keepdims=True), sink)
    p = jnp.exp(s - m)
    den = jnp.sum(p, axis=1, keepdims=True) + jnp.exp(sink - m)
    return jnp.dot(p.astype(BF16), v, preferred_element_type=F32) / den


def _swa_prompt_kernel(q_ref, kp_ref, kc_ref, vp_ref, vc_ref, sink_ref, o_ref):
    qb = pl.program_id(1)
    k = jnp.concatenate([kp_ref[...], kc_ref[...]], axis=0).astype(BF16)
    v = jnp.concatenate([vp_ref[...], vc_ref[...]], axis=0).astype(BF16)
    qrow = lax.broadcasted_iota(I32, (WINDOW, 2 * WINDOW), 0)
    kcol = lax.broadcasted_iota(I32, (WINDOW, 2 * WINDOW), 1)
    first = jnp.where(qb > 0, 0, WINDOW)
    ok = (kcol > qrow) & (kcol <= qrow + WINDOW) & (kcol >= first)
    bias = jnp.where(ok, 0.0, NEG_INF)
    nkeys = 2 * WINDOW
    zeros = jnp.zeros((nkeys, B_DH), BF16)
    lane = lax.broadcasted_iota(I32, (WINDOW, 2 * B_DH), 1)
    g = B_HEADS // B_KV

    def block_diag(x):
        return jnp.concatenate([jnp.concatenate([x, zeros], axis=1), jnp.concatenate([zeros, x], axis=1)], axis=0)

    pairs = [(kv, kv * g + 2 * i) for kv in range(B_KV) for i in range(g // 2)]
    k2 = [block_diag(k[:, kv * B_DH:(kv + 1) * B_DH]) for kv in range(B_KV)]
    v2 = [block_diag(v[:, kv * B_DH:(kv + 1) * B_DH]) for kv in range(B_KV)]
    scores = [lax.dot_general(q_ref[:, h0 * B_DH:(h0 + 2) * B_DH].astype(BF16), k2[kv], (((1,), (1,)), ((), ())),
                              preferred_element_type=F32) for kv, h0 in pairs]
    probs, dens = [], []
    for (kv, h0), s2 in zip(pairs, scores):
        ps, ds = [], []
        for e in range(2):
            s = s2[:, e * nkeys:(e + 1) * nkeys] * (B_DH ** -0.5) + bias
            sink = sink_ref[h0 + e:h0 + e + 1, :]
            m = jnp.maximum(jnp.max(s, axis=1, keepdims=True), sink)
            p = jnp.exp(s - m)
            ps.append(p.astype(BF16))
            ds.append(jnp.sum(p, axis=1, keepdims=True) + jnp.exp(sink - m))
        probs.append(jnp.concatenate(ps, axis=1))
        dens.append(jnp.where(lane < B_DH, ds[0], ds[1]))
    for (kv, h0), p2, den in zip(pairs, probs, dens):
        o2 = jnp.dot(p2, v2[kv], preferred_element_type=F32) / den
        o_ref[:, h0 * B_DH:(h0 + 2) * B_DH] = o2.astype(BF16)


def _swa_prompt(pb, sinks):
    nq = SEQ // WINDOW
    kvw = B_KV * B_DH
    kblk, vblk = D // kvw, D // kvw + 1

    def prev(b, i):
        return b * nq + jnp.maximum(i - 1, 0)

    return pl.pallas_call(
        _swa_prompt_kernel,
        grid=(BATCH, nq),
        in_specs=[pl.BlockSpec((WINDOW, D), lambda b, i: (b * nq + i, 0)),
                  pl.BlockSpec((WINDOW, kvw), lambda b, i: (prev(b, i), kblk)),
                  pl.BlockSpec((WINDOW, kvw), lambda b, i: (b * nq + i, kblk)),
                  pl.BlockSpec((WINDOW, kvw), lambda b, i: (prev(b, i), vblk)),
                  pl.BlockSpec((WINDOW, kvw), lambda b, i: (b * nq + i, vblk)),
                  pl.BlockSpec((B_HEADS, 1), lambda b, i: (0, 0))],
        out_specs=pl.BlockSpec((WINDOW, D), lambda b, i: (b * nq + i, 0)),
        out_shape=jax.ShapeDtypeStruct((NP, D), BF16),
        compiler_params=_cp("arbitrary", "arbitrary"),
        name="swa_prompt",
    )(pb, pb, pb, pb, pb, sinks.reshape(B_HEADS, 1))


def _swa_sample_kernel(q_ref, kn_ref, vn_ref, kbuf_ref, vbuf_ref, sink_ref, o_ref, ko_ref, vo_ref):
    row = lax.broadcasted_iota(I32, (WINDOW, B_KV * B_DH), 0)
    k = jnp.where(row == WINDOW - 1, kn_ref[0], pltpu.roll(kbuf_ref[0], WINDOW - 1, 0))
    v = jnp.where(row == WINDOW - 1, vn_ref[0], pltpu.roll(vbuf_ref[0], WINDOW - 1, 0))
    ko_ref[0] = k
    vo_ref[0] = v
    kb, vb = k.astype(BF16), v.astype(BF16)
    g = B_HEADS // B_KV
    for kv in range(B_KV):
        qg = q_ref[0, kv * g:(kv + 1) * g, :].astype(BF16)
        s = lax.dot_general(qg, kb[:, kv * B_DH:(kv + 1) * B_DH], (((1,), (1,)), ((), ())),
                            preferred_element_type=F32) * (B_DH ** -0.5)
        o_ref[0, kv * g:(kv + 1) * g, :] = _sink_softmax_pv(s, sink_ref[kv * g:(kv + 1) * g, :],
                                                            vb[:, kv * B_DH:(kv + 1) * B_DH])


def _swa_sample(q, k_new, v_new, kbuf, vbuf, sinks):
    kvw = B_KV * B_DH
    state = jax.ShapeDtypeStruct((DEC_BATCH, WINDOW, kvw), F32)
    per_b3 = lambda shape: pl.BlockSpec((1,) + shape, lambda b: (b, 0, 0))
    return pl.pallas_call(
        _swa_sample_kernel,
        grid=(DEC_BATCH,),
        in_specs=[per_b3((B_HEADS, B_DH)), per_b3((1, kvw)), per_b3((1, kvw)),
                  per_b3((WINDOW, kvw)), per_b3((WINDOW, kvw)),
                  pl.BlockSpec((B_HEADS, 1), lambda b: (0, 0))],
        out_specs=[per_b3((B_HEADS, B_DH)), per_b3((WINDOW, kvw)), per_b3((WINDOW, kvw))],
        out_shape=[jax.ShapeDtypeStruct((DEC_BATCH, B_HEADS, B_DH), F32), state, state],
        compiler_params=_cp("arbitrary"),
        name="swa_sample",
    )(q, k_new, v_new, kbuf, vbuf, sinks.reshape(B_HEADS, 1))


def _out_proj_kernel(o_ref, w_ref, x_ref, gate_ref, xo_ref):
    y = jnp.dot(o_ref[...], w_ref[0].astype(BF16), preferred_element_type=F32)
    xo_ref[...] = x_ref[...] + gate_ref[0] * y


def _out_proj(x, o, rows, w, layer, name):
    tm, tn = rows.tm, 512
    return pl.pallas_call(
        _out_proj_kernel,
        grid=(rows.n, D // tn),
        in_specs=[pl.BlockSpec((tm, D), lambda i, j: (i, 0)),
                  pl.BlockSpec((1, D, tn), lambda i, j: (layer, 0, j)),
                  pl.BlockSpec((tm, tn), lambda i, j: (i + rows.off, j)),
                  _mod_spec(rows, 2, width=tn, col=True)],
        out_specs=pl.BlockSpec((tm, tn), lambda i, j: (i + rows.off, j)),
        out_shape=jax.ShapeDtypeStruct((RP, D), F32),
        input_output_aliases={2: 0},
        compiler_params=_cp("arbitrary", "arbitrary"),
        name=name,
    )(o, w, x, rows.mod)


ROW_TM = SAMPLE_ROWS
N_PROMPT_BLOCKS = NP // ROW_TM


def _both_mod_specs(chunk):
    def prompt_idx(i, *_):
        return (jnp.minimum(i, N_PROMPT_BLOCKS - 1) * ROW_TM // SEQ, 0, chunk)

    return [pl.BlockSpec((1, 1, D), prompt_idx), pl.BlockSpec((1, ROW_TM, D), lambda i, *_: (0, 0, chunk))]


def _by_row_kind(fn, *pairs):
    i = pl.program_id(0)

    @pl.when(i < N_PROMPT_BLOCKS)
    def _():
        fn(*[p[0] for p, _ in pairs])

    @pl.when(i >= N_PROMPT_BLOCKS)
    def _():
        fn(*[s[0] for _, s in pairs])


def _ffn_pre_kernel(x_ref, g_ref, shp_ref, shs_ref, scp_ref, scs_ref, wr_ref, *out_refs, moe):
    _by_row_kind(functools.partial(_ffn_pre_rows, x_ref, g_ref, wr_ref, out_refs, moe),
                 (shp_ref, shs_ref), (scp_ref, scs_ref))


def _ffn_pre_rows(x_ref, g_ref, wr_ref, out_refs, moe, shift, scale):
    h = _modulated(x_ref[...], g_ref[...], shift, scale)
    out_refs[0][...] = h
    if not moe:
        return
    logits = jnp.dot(h, wr_ref[0], preferred_element_type=F32, precision=lax.Precision.HIGHEST)
    lane = lax.broadcasted_iota(I32, logits.shape, 1).astype(F32)
    l1 = jnp.where(lane < N_EXP, logits, NEG_INF)
    m1 = jnp.max(l1, axis=1, keepdims=True)
    i1 = jnp.min(jnp.where(l1 == m1, lane, float(LANES)), axis=1, keepdims=True)
    l2 = jnp.where(lane == i1, NEG_INF, l1)
    m2 = jnp.max(l2, axis=1, keepdims=True)
    i2 = jnp.min(jnp.where(l2 == m2, lane, float(LANES)), axis=1, keepdims=True)
    e2 = jnp.exp(m2 - m1)
    g1 = 1.0 / (1.0 + e2)
    g2 = e2 / (1.0 + e2)
    out_refs[1][...] = jnp.where(lane == 0, i1, jnp.where(lane == 1, i2, jnp.where(
        lane == 2, g1, jnp.where(lane == 3, g2, 0.0))))


def _ffn_pre(x, g, mod_p, mod_s, wr, layer):
    moe = wr is not None
    row_blk = lambda w: pl.BlockSpec((ROW_TM, w), lambda i: (i, 0))
    shift, scale = _both_mod_specs(3), _both_mod_specs(4)
    in_specs = [row_blk(D), pl.BlockSpec((1, D), lambda i: (0, 0)), *shift, *scale,
                pl.BlockSpec((1, D, LANES), lambda i: (layer if moe else 0, 0, 0))]
    out_specs = [row_blk(D)] + ([row_blk(LANES)] if moe else [])
    out_shape = [jax.ShapeDtypeStruct((RP, D), F32)] + ([jax.ShapeDtypeStruct((RP, LANES), F32)] if moe else [])
    return pl.pallas_call(
        functools.partial(_ffn_pre_kernel, moe=moe), grid=(RP // ROW_TM,), in_specs=in_specs,
        out_specs=out_specs, out_shape=out_shape, compiler_params=_cp("arbitrary"), name="ffn_pre",
    )(x, g.reshape(1, D), mod_p, mod_s, mod_p, mod_s, wr if moe else jnp.zeros((1, D, LANES), F32))


def _moe_plan(route):
    e = route[:N_TOK, :2].astype(I32).reshape(-1)
    oh = (e[:, None] == jnp.arange(N_EXP, dtype=I32)[None, :]).astype(I32)
    csum = jnp.cumsum(oh, axis=0)
    rank = jnp.take_along_axis(csum - oh, e[:, None], axis=1)[:, 0]
    counts = csum[-1]
    ntile = (counts + FFN_TM - 1) // FFN_TM
    tend = jnp.cumsum(ntile)
    tstart = tend - ntile
    pos = tstart[e] * FFN_TM + rank
    src = jnp.zeros((MOE_TILES * FFN_TM,), I32).at[pos].set(jnp.arange(2 * N_TOK, dtype=I32) // 2)
    t = jnp.arange(MOE_TILES, dtype=I32)
    te = jnp.minimum(jnp.sum((t[:, None] >= tend[None, :]).astype(I32), axis=1), N_EXP - 1)
    total = tend[-1]
    nrows = jnp.where(t < total, jnp.clip(counts[te] - (t - tstart[te]) * FFN_TM, 0, FFN_TM), 0)
    te = jnp.where(t < total, te, te[total - 1])
    p = pos.reshape(N_TOK, 2)
    p = jnp.concatenate([p, jnp.broadcast_to(p[:1], (RP - N_TOK, 2))], axis=0)
    return dict(src=src, te=te, nrows=nrows.astype(I32), p1=p[:, 0], p2=p[:, 1])


N_F = FFN // FFN_TF
GATHER_PER_STEP = -(-FFN_TM // N_F)
GATHER_COPIES = GATHER_PER_STEP * N_F


def _ffn_kernel(te_ref, nr_ref, src_ref, h_hbm, wg_ref, wu_ref, wd_ref, o_ref, xf, xb, sem, *, n_tiles):
    t = pl.program_id(0)
    f = pl.program_id(1)
    n_chunks = (nr_ref[t] + FFN_CH - 1) // FFN_CH

    def row_copy(tile, k):
        r = jnp.minimum(k, FFN_TM - 1)
        return pltpu.make_async_copy(h_hbm.at[pl.ds(src_ref[tile * FFN_TM + r], 1)], xf.at[pl.ds(k, 1)], sem)

    def wait_tile():
        def wait(k, c):
            row_copy(0, k).wait()
            return c

        lax.fori_loop(0, GATHER_COPIES, wait, 0)

    @pl.when((t == 0) & (f == 0))
    def _():
        def issue(k, c):
            row_copy(0, k).start()
            return c

        lax.fori_loop(0, GATHER_COPIES, issue, 0)

    @pl.when(f == 0)
    def _():
        wait_tile()
        xb[...] = xf[0:FFN_TM, :].astype(BF16)
        o_ref[...] = jnp.zeros_like(o_ref)

    nxt = jnp.minimum(t + 1, n_tiles - 1)

    def fetch_next():
        for r in range(GATHER_PER_STEP):
            row_copy(nxt, f * GATHER_PER_STEP + r).start()

    for k in range(1, N_CH + 1):
        @pl.when(n_chunks == k)
        def _():
            fetch_next()
            rows = k * FFN_CH
            xc = xb[0:rows, :]
            gate = jnp.dot(xc, wg_ref[0, 0].astype(BF16), preferred_element_type=F32)
            up = jnp.dot(xc, wu_ref[0, 0].astype(BF16), preferred_element_type=F32)
            act = ((gate / (1.0 + jnp.exp(-gate))) * up).astype(BF16)
            o_ref[0:rows, :] += jnp.dot(act, wd_ref[0, 0].astype(BF16), preferred_element_type=F32)

    @pl.when(n_chunks == 0)
    def _():
        fetch_next()

    @pl.when((t == n_tiles - 1) & (f == N_F - 1))
    def _():
        wait_tile()


def _ffn(h, src, te, nrows, wg, wu, wd, layer, n_tiles):
    def fcol(t, f, nr):
        return jnp.where(nr[t] > 0, f, N_F - 1)

    grid_spec = pltpu.PrefetchScalarGridSpec(
        num_scalar_prefetch=3,
        grid=(n_tiles, N_F),
        in_specs=[pl.BlockSpec(memory_space=pl.ANY),
                  pl.BlockSpec((1, 1, D, FFN_TF), lambda t, f, te, nr, src: (layer, te[t], 0, fcol(t, f, nr))),
                  pl.BlockSpec((1, 1, D, FFN_TF), lambda t, f, te, nr, src: (layer, te[t], 0, fcol(t, f, nr))),
                  pl.BlockSpec((1, 1, FFN_TF, D), lambda t, f, te, nr, src: (layer, te[t], fcol(t, f, nr), 0))],
        out_specs=pl.BlockSpec((FFN_TM, D), lambda t, f, te, nr, src: (t, 0)),
        scratch_shapes=[pltpu.VMEM((-(-GATHER_COPIES // 8) * 8, D), F32), pltpu.VMEM((FFN_TM, D), BF16),
                        pltpu.SemaphoreType.DMA])
    return pl.pallas_call(
        functools.partial(_ffn_kernel, n_tiles=n_tiles), grid_spec=grid_spec,
        out_shape=jax.ShapeDtypeStruct((n_tiles * FFN_TM, D), F32),
        compiler_params=_cp("arbitrary", "arbitrary"), name="ffn",
    )(te, nrows, src, h, wg, wu, wd)


def _dense_residual_kernel(x_ref, y_ref, gp_ref, gs_ref, o_ref):
    def rows(gate):
        o_ref[...] = x_ref[...] + gate * y_ref[...]

    _by_row_kind(rows, (gp_ref, gs_ref))


def _dense_residual(x, y, mod_p, mod_s):
    blk = pl.BlockSpec((ROW_TM, D), lambda i: (i, 0))
    return pl.pallas_call(
        _dense_residual_kernel, grid=(RP // ROW_TM,),
        in_specs=[blk, blk, *_both_mod_specs(5)], out_specs=blk,
        out_shape=jax.ShapeDtypeStruct((RP, D), F32), input_output_aliases={0: 0},
        compiler_params=_cp("arbitrary"), name="dense_residual",
    )(x, y, mod_p, mod_s)


def _moe_combine_kernel(p1_ref, p2_ref, x_ref, r_ref, gp_ref, gs_ref, y_hbm, o_ref, ybuf, sem):
    base = pl.program_id(0) * ROW_TM

    def row_copy(r, k, p):
        return pltpu.make_async_copy(y_hbm.at[pl.ds(p, 1)], ybuf.at[k, pl.ds(r, 1)], sem.at[k])

    def issue(r, c):
        row_copy(r, 0, p1_ref[base + r]).start()
        row_copy(r, 1, p2_ref[base + r]).start()
        return c

    lax.fori_loop(0, ROW_TM, issue, 0)

    def wait(r, c):
        row_copy(r, 0, 0).wait()
        row_copy(r, 1, 0).wait()
        return c

    lax.fori_loop(0, ROW_TM, wait, 0)

    def rows(gate):
        mix = r_ref[:, 2:3] * ybuf[0] + r_ref[:, 3:4] * ybuf[1]
        o_ref[...] = x_ref[...] + gate * mix

    _by_row_kind(rows, (gp_ref, gs_ref))


def _moe_combine(x, y, route, plan, mod_p, mod_s):
    blk = lambda w: pl.BlockSpec((ROW_TM, w), lambda i, p1, p2: (i, 0))
    grid_spec = pltpu.PrefetchScalarGridSpec(
        num_scalar_prefetch=2,
        grid=(RP // ROW_TM,),
        in_specs=[blk(D), blk(LANES), *_both_mod_specs(5), pl.BlockSpec(memory_space=pl.ANY)],
        out_specs=blk(D),
        scratch_shapes=[pltpu.VMEM((2, ROW_TM, D), F32), pltpu.SemaphoreType.DMA((2,))])
    return pl.pallas_call(
        _moe_combine_kernel, grid_spec=grid_spec,
        out_shape=jax.ShapeDtypeStruct((RP, D), F32), input_output_aliases={2: 0},
        compiler_params=_cp("arbitrary"), name="moe_combine",
    )(plan["p1"], plan["p2"], x, route, mod_p, mod_s, y)


def _final_norm_kernel(x_ref, g_ref, o_ref):
    x = x_ref[...]
    ms = jnp.mean(x * x, axis=-1, keepdims=True)
    o_ref[...] = x * lax.rsqrt(ms + EPS) * g_ref[...]


def _final_norm(x, g):
    tm = 256
    return pl.pallas_call(
        _final_norm_kernel, grid=(RP // tm,),
        in_specs=[pl.BlockSpec((tm, D), lambda i: (i, 0)), pl.BlockSpec((1, D), lambda i: (0, 0))],
        out_specs=pl.BlockSpec((tm, D), lambda i: (i, 0)),
        out_shape=jax.ShapeDtypeStruct((RP, D), F32),
        compiler_params=_cp("arbitrary"), name="final_norm",
    )(x, g.reshape(1, D))


A_TILES = (T_ROPE128,) * 5 + (T_PLAIN,) + (T_ROPE64,) * 2 + (T_KIWI,)
B_TILES = (T_ROPE64,) * 9 + (T_PLAIN,)


def kernel(x_prompt, x_sample, cache_k_a, cache_v_a, cache_kidx_a, state_k_b, state_v_b, page_table,
           c_prompt, c_sample, w_ada, b_ada, g_attn, g_ffn, w_in_a, w_out_a, w_in_b, w_out_b, sinks_b,
           w_gate_d, w_up_d, w_down_d, w_router, w_gate_e, w_up_e, w_down_e, g_final):
    x = jnp.concatenate([x_prompt.reshape(NP, D), x_sample.reshape(DEC_BATCH, D),
                         jnp.zeros((SAMPLE_ROWS - DEC_BATCH, D), F32)], axis=0)
    c_all = jnp.concatenate([c_prompt, c_sample, jnp.zeros((4, D), F32)], axis=0)
    mods = _adaln(c_all, w_ada, b_ada)
    mod_p = mods[:, :BATCH].reshape(DEPTH, BATCH, 1, 6 * D)
    mod_s = jnp.pad(mods[:, BATCH:BATCH + DEC_BATCH], ((0, 0), (0, SAMPLE_ROWS - DEC_BATCH), (0, 0)))
    mod_s = mod_s.reshape(DEPTH, 1, SAMPLE_ROWS, 6 * D)

    tab_p = _rope_tables(np.arange(SEQ))
    tab_s = _rope_tables(np.full((SAMPLE_ROWS,), PAST))
    w_in_a_pad = jnp.pad(w_in_a, ((0, 0), (0, 0), (0, A_PROJ_PAD - A_PROJ)))
    w_router_pad = jnp.pad(w_router, ((0, 0), (0, 0), (0, LANES - N_EXP)))
    kvw = B_KV * B_DH
    cache_kidx_t = jnp.swapaxes(cache_kidx_a, 2, 3)
    dense_src = jnp.asarray(np.minimum(np.arange(DENSE_TILES * FFN_TM), RP - 1).astype(np.int32))
    dense_te = jnp.zeros((DENSE_TILES,), I32)
    dense_rows = jnp.asarray(np.array([min(FFN_TM, RP - t * FFN_TM) for t in range(DENSE_TILES)], np.int32))

    ka_p, va_p, ia_p, ka_s, va_s, ia_s, kb_p, vb_p, kb_s, vb_s = ([] for _ in range(10))
    for i in range(DEPTH):
        j = i // 2
        rp = lambda tm: _Rows(tm, NP // tm, 0, mod_p[i], False)
        rs = _Rows(SAMPLE_ROWS, 1, NP // SAMPLE_ROWS, mod_s[i], True)
        if i % 2 == 0:
            pa = _proj(x, rp(1024), g_attn[i], 0, w_in_a_pad, j, tab_p, A_TILES, 512, "proj_a_prompt")
            sa = _proj(x, rs, g_attn[i], 0, w_in_a_pad, j, tab_s, A_TILES, 512, "proj_a_sample")
            ka_p.append(pa[:, 2048:2560].reshape(BATCH, SEQ, A_KV, A_DH))
            va_p.append(pa[:, 2560:3072].reshape(BATCH, SEQ, A_KV, A_DH))
            ia_p.append(pa[:, 4096:4160].reshape(BATCH, SEQ, IDX_DIM))
            sa = sa[:DEC_BATCH]
            k_new = sa[:, 2048:2560].reshape(DEC_BATCH, A_KV, A_DH)
            v_new = sa[:, 2560:3072].reshape(DEC_BATCH, A_KV, A_DH)
            ki_new = sa[:, 4096:4160].reshape(DEC_BATCH, 1, IDX_DIM)
            ka_s.append(k_new.reshape(DEC_BATCH, 1, A_KV, A_DH))
            va_s.append(v_new.reshape(DEC_BATCH, 1, A_KV, A_DH))
            ia_s.append(ki_new)
            o_p = _dsa_prompt(pa)
            scores = _dsa_score(page_table, sa[:, 3072:4096].reshape(DEC_BATCH, IDX_HEADS, IDX_DIM),
                                sa[:, 4160:4176].reshape(DEC_BATCH, IDX_HEADS, 1),
                                ki_new.reshape(DEC_BATCH, IDX_DIM, 1), cache_kidx_t, j)
            sel = _dsa_select(scores)
            o_s = _dsa_attend(sel, page_table, sa[:, :2048].reshape(DEC_BATCH, A_HEADS, A_DH), k_new, v_new,
                              cache_k_a, cache_v_a, j)
            w_out = w_out_a
        else:
            pb = _proj(x, rp(1024), g_attn[i], 0, w_in_b, j, tab_p, B_TILES, 256, "proj_b_prompt")
            sb = _proj(x, rs, g_attn[i], 0, w_in_b, j, tab_s, B_TILES, 256, "proj_b_sample")
            kb_p.append(pb[:, D:D + kvw].reshape(BATCH, SEQ, B_KV, B_DH)[:, SEQ - WINDOW:])
            vb_p.append(pb[:, D + kvw:].reshape(BATCH, SEQ, B_KV, B_DH)[:, SEQ - WINDOW:])
            sb = sb[:DEC_BATCH]
            o_p = _swa_prompt(pb, sinks_b[j])
            o_s, k_st, v_st = _swa_sample(sb[:, :D].reshape(DEC_BATCH, B_HEADS, B_DH),
                                          sb[:, D:D + kvw].reshape(DEC_BATCH, 1, kvw),
                                          sb[:, D + kvw:].reshape(DEC_BATCH, 1, kvw),
                                          state_k_b[j].reshape(DEC_BATCH, WINDOW, kvw),
                                          state_v_b[j].reshape(DEC_BATCH, WINDOW, kvw), sinks_b[j])
            kb_s.append(k_st.reshape(DEC_BATCH, WINDOW, B_KV, B_DH))
            vb_s.append(v_st.reshape(DEC_BATCH, WINDOW, B_KV, B_DH))
            w_out = w_out_b
        o_s = jnp.pad(o_s.reshape(DEC_BATCH, D), ((0, SAMPLE_ROWS - DEC_BATCH), (0, 0))).astype(BF16)
        x = _out_proj(x, o_p, rp(1024), w_out, j, "out_proj_prompt")
        x = _out_proj(x, o_s, rs, w_out, j, "out_proj_sample")

        if i % 2 == 0:
            (h,) = _ffn_pre(x, g_ffn[i], mod_p[i], mod_s[i], None, j)
            y = _ffn(h, dense_src, dense_te, dense_rows, w_gate_d[:, None], w_up_d[:, None], w_down_d[:, None],
                     j, DENSE_TILES)
            x = _dense_residual(x, y, mod_p[i], mod_s[i])
        else:
            h, route = _ffn_pre(x, g_ffn[i], mod_p[i], mod_s[i], w_router_pad, j)
            plan = _moe_plan(route)
            y = _ffn(h, plan["src"], plan["te"], plan["nrows"], w_gate_e, w_up_e, w_down_e, j, MOE_TILES)
            x = _moe_combine(x, y, route, plan, mod_p[i], mod_s[i])

    y = _final_norm(x, g_final)
    return (y[:NP].reshape(BATCH, SEQ, D), y[NP:N_TOK].reshape(DEC_BATCH, 1, D),
            jnp.stack(ka_p), jnp.stack(va_p), jnp.stack(ia_p),
            jnp.stack(ka_s), jnp.stack(va_s), jnp.stack(ia_s),
            jnp.stack(kb_p), jnp.stack(vb_p), jnp.stack(kb_s), jnp.stack(vb_s))
```

```python
import functools
from typing import NamedTuple

import numpy as np
import jax
import jax.numpy as jnp
from jax import lax
from jax.experimental import pallas as pl
from jax.experimental.pallas import tpu as pltpu

F32, BF16, I32 = jnp.float32, jnp.bfloat16, jnp.int32

D = 2048
BATCH, SEQ = 4, 2048
DEPTH = 4
DEC_BATCH = 32
PAST = 16384
PAGE = 128
N_PAGES = PAST // PAGE
A_HEADS, A_KV, A_DH = 16, 4, 128
IDX_HEADS, IDX_DIM = 16, 64
TOPK = 256
B_HEADS, B_KV, B_DH = 32, 4, 64
WINDOW = 128
FFN = 7168
N_EXP = 8
ROPE_THETA = 500000.0
EPS = 1e-6
A_PROJ = 4176
A_PROJ_PAD = 4608
B_PROJ = 2560

NP = BATCH * SEQ
SAMPLE_ROWS = 256
RP = NP + SAMPLE_ROWS
N_TOK = NP + DEC_BATCH

VMEM_LIMIT = 56 * 1024 * 1024
LANES = 128

NEG_INF = float("-inf")
INT_MIN = -2147483648
KEY_NEG_INF = -2139095041

FFN_TM = 1152
FFN_CH = 128
FFN_TF = 256
N_CH = FFN_TM // FFN_CH
MOE_TILES = (2 * N_TOK) // FFN_TM + N_EXP
DENSE_TILES = -(-RP // FFN_TM)

SCORE_N = (N_PAGES + 1) * PAGE


def _cp(*sem):
    return pltpu.CompilerParams(dimension_semantics=sem, vmem_limit_bytes=VMEM_LIMIT)


class _Rows(NamedTuple):
    tm: int
    n: int
    off: int
    mod: jax.Array
    per_row: bool


def _mod_spec(rows, chunk, width=D, col=None):
    per_blk = width if col is not None else D
    nblk = D // per_blk

    def idx(i, *rest):
        j = rest[0] if col is not None else 0
        c = chunk * nblk + j
        if rows.per_row:
            return (0, 0, c)
        return ((i * rows.tm) // SEQ, 0, c)

    shape = (1, rows.tm if rows.per_row else 1, per_blk)
    return pl.BlockSpec(shape, idx)


def _adaln_kernel(c_ref, w_ref, b_ref, o_ref):
    c = c_ref[...]
    a = (c / (1.0 + jnp.exp(-c))).astype(BF16)
    o_ref[0] = jnp.dot(a, w_ref[0].astype(BF16), preferred_element_type=F32) + b_ref[0]


def _adaln(c_all, w_ada, b_ada):
    tn = 1024
    rows = c_all.shape[0]
    return pl.pallas_call(
        _adaln_kernel,
        grid=(DEPTH, 6 * D // tn),
        in_specs=[pl.BlockSpec((rows, D), lambda l, j: (0, 0)),
                  pl.BlockSpec((1, D, tn), lambda l, j: (l, 0, j)),
                  pl.BlockSpec((1, 1, tn), lambda l, j: (l, 0, j))],
        out_specs=pl.BlockSpec((1, rows, tn), lambda l, j: (l, 0, j)),
        out_shape=jax.ShapeDtypeStruct((DEPTH, rows, 6 * D), F32),
        compiler_params=_cp("arbitrary", "arbitrary"),
        name="adaln",
    )(c_all, w_ada, b_ada.reshape(DEPTH, 1, 6 * D))


T_ROPE128, T_PLAIN, T_ROPE64, T_KIWI = 0, 1, 2, 3


def _rope_tables(pos):
    pos = jnp.asarray(pos, F32)[:, None]
    col = np.arange(LANES)

    def kind(head_dim, ncols):
        rot = head_dim // 4
        half = rot // 2
        inv = ROPE_THETA ** (-jnp.arange(half, dtype=F32) * 2.0 / rot)
        ang = pos * inv[None, :]
        cos, sin = jnp.cos(ang), jnp.sin(ang)
        c = col % head_dim
        lo = (c < half) & (col < ncols)
        hi = (c >= half) & (c < rot) & (col < ncols)
        f = np.where(lo, c, np.where(hi, c - half, 0))
        one = (col < ncols).astype(np.float32)
        cc = jnp.where(lo | hi, cos[:, f], one[None, :])
        aa = jnp.where(lo, -sin[:, f], 0.0)
        bb = jnp.where(hi, sin[:, f], 0.0)
        return cc, aa, bb

    k128 = kind(A_DH, LANES)
    k64 = kind(IDX_DIM, LANES)
    plain = (jnp.ones_like(k128[0]), jnp.zeros_like(k128[0]), jnp.zeros_like(k128[0]))
    kw = list(kind(IDX_DIM, IDX_DIM))
    wi_cols = (col >= IDX_DIM) & (col < IDX_DIM + IDX_HEADS)
    kw[0] = jnp.where(wi_cols[None, :], IDX_HEADS ** -0.5 * IDX_DIM ** -0.5, kw[0])
    kinds = [k128, plain, k64, tuple(kw)]
    return tuple(jnp.stack([k[t] for k in kinds]) for t in range(3))


def _modulated(x, g, shift, scale):
    ms = jnp.mean(x * x, axis=-1, keepdims=True)
    return (x * lax.rsqrt(ms + EPS) * g) * (1.0 + scale) + shift


def _proj_kernel(tt_ref, x_ref, g_ref, sh_ref, sc_ref, w_ref, c_ref, a_ref, b_ref, o_ref, h_ref, *, tn):
    j = pl.program_id(1)

    @pl.when(j == 0)
    def _():
        h_ref[...] = _modulated(x_ref[...], g_ref[...], sh_ref[0], sc_ref[0]).astype(BF16)

    y = jnp.dot(h_ref[...], w_ref[0].astype(BF16), preferred_element_type=F32)
    t = tt_ref[j]
    @pl.when(t == T_PLAIN)
    def _():
        o_ref[...] = y

    def rope(half):
        c, a, b = c_ref[0], a_ref[0], b_ref[0]
        for blk in range(tn // LANES):
            cols = slice(blk * LANES, (blk + 1) * LANES)
            yb = y[:, cols]
            o_ref[:, cols] = yb * c + pltpu.roll(yb, LANES - half, 1) * a + pltpu.roll(yb, half, 1) * b

    @pl.when(t == T_ROPE128)
    def _():
        rope(A_DH // 8)

    @pl.when(t >= T_ROPE64)
    def _():
        rope(IDX_DIM // 8)


def _proj(x, rows, g, shift_chunk, w, layer, tables, tile_types, tn, name):
    tm = rows.tm
    nj = w.shape[2] // tn
    tt = jnp.asarray(np.asarray(tile_types, np.int32))
    pos_blocks = tables[0].shape[1] // tm
    tab_spec = pl.BlockSpec((1, tm, LANES), lambda i, j, tt: (tt[j], i % pos_blocks, 0))
    grid_spec = pltpu.PrefetchScalarGridSpec(
        num_scalar_prefetch=1,
        grid=(rows.n, nj),
        in_specs=[pl.BlockSpec((tm, D), lambda i, j, tt: (i + rows.off, 0)),
                  pl.BlockSpec((1, D), lambda i, j, tt: (0, 0)),
                  _mod_spec(rows, shift_chunk),
                  _mod_spec(rows, shift_chunk + 1),
                  pl.BlockSpec((1, D, tn), lambda i, j, tt: (layer, 0, j)),
                  tab_spec, tab_spec, tab_spec],
        out_specs=pl.BlockSpec((tm, tn), lambda i, j, tt: (i, j)),
        scratch_shapes=[pltpu.VMEM((tm, D), BF16)])
    return pl.pallas_call(
        functools.partial(_proj_kernel, tn=tn),
        grid_spec=grid_spec,
        out_shape=jax.ShapeDtypeStruct((rows.n * tm, nj * tn), F32),
        compiler_params=_cp("arbitrary", "arbitrary"),
        name=name,
    )(tt, x, g.reshape(1, D), rows.mod, rows.mod, w, *tables)


def _ordered_key(x):
    b = pltpu.bitcast(x, I32)
    return b ^ ((b >> 31) & 0x7FFFFFFF)


def _select_topk(score, valid, pos, k, idx_bits, j_ref):
    rows = score.shape[0]
    key = jnp.where(valid, _ordered_key(score), KEY_NEG_INF)

    def count(m):
        return jnp.sum(jnp.where(m, 1.0, 0.0), axis=1, keepdims=True)

    def value_step(i, cand):
        trial = cand + jnp.left_shift(jnp.int32(1), 31 - i)
        return jnp.where(count(key >= trial) >= k, trial, cand)

    thr = lax.fori_loop(0, 32, value_step, jnp.full((rows, 1), INT_MIN, I32))
    gt = key > thr
    eq = key == thr
    need = k - count(gt)
    tied = jnp.where(count(eq) > need, jnp.where(thr > KEY_NEG_INF, 1.0, 0.0), 0.0)
    j_ref[...] = jnp.full((rows, 1), 1 << idx_bits, I32)

    @pl.when(jnp.max(tied) > 0.0)
    def _():
        def index_step(i, j):
            trial = j + jnp.left_shift(jnp.int32(1), idx_bits - 1 - i)
            below = jnp.where(eq, jnp.where(pos < trial, 1.0, 0.0), 0.0)
            return jnp.where(jnp.sum(below, axis=1, keepdims=True) < need, trial, j)

        j_ref[...] = lax.fori_loop(0, idx_bits, index_step, jnp.zeros((rows, 1), I32))

    return valid & (gt | (eq & (pos <= j_ref[...])))


def _dsa_prompt_body(nk, qb, q_ref, qi_ref, wq_ref, o_ref, kb, vb, kib, j_ref, tq):
    ki = kib[0:nk, :]
    score = jnp.zeros((tq, nk), F32)
    for h in range(IDX_HEADS):
        qh = qi_ref[:, h * IDX_DIM:(h + 1) * IDX_DIM].astype(BF16)
        s = lax.dot_general(qh, ki, (((1,), (1,)), ((), ())), preferred_element_type=F32)
        score = score + wq_ref[:, IDX_DIM + h:IDX_DIM + h + 1] * jnp.maximum(s, 0.0)

    kpos = lax.broadcasted_iota(I32, (tq, nk), 1)
    qpos = qb * tq + lax.broadcasted_iota(I32, (tq, nk), 0)
    sel = _select_topk(score, kpos <= qpos, kpos, min(TOPK, SEQ // 4), 11, j_ref)
    bias = jnp.where(sel, 0.0, NEG_INF)

    for h in range(A_HEADS):
        kv = h // (A_HEADS // A_KV)
        qh = q_ref[:, h * A_DH:(h + 1) * A_DH].astype(BF16)
        s = lax.dot_general(qh, kb[0:nk, kv * A_DH:(kv + 1) * A_DH], (((1,), (1,)), ((), ())),
                            preferred_element_type=F32) * (A_DH ** -0.5) + bias
        p = jnp.exp(s - jnp.max(s, axis=1, keepdims=True))
        ov = jnp.dot(p.astype(BF16), vb[0:nk, kv * 2 * A_DH:(kv + 1) * 2 * A_DH], preferred_element_type=F32)
        o_ref[:, h * A_DH:(h + 1) * A_DH] = (ov[:, 0:A_DH] / ov[:, A_DH:2 * A_DH]).astype(BF16)


def _dsa_prompt_kernel(q_ref, qi_ref, wq_ref, k_ref, v_ref, kw_ref, o_ref, kb, vb, kib, j_ref, *, tq, key_steps):
    qb = pl.program_id(1)

    @pl.when(qb == 0)
    def _():
        kb[...] = k_ref[...].astype(BF16)
        for kv in range(A_KV):
            vb[:, 2 * kv * A_DH:(2 * kv + 1) * A_DH] = v_ref[:, kv * A_DH:(kv + 1) * A_DH].astype(BF16)
            vb[:, (2 * kv + 1) * A_DH:(2 * kv + 2) * A_DH] = jnp.ones((SEQ, A_DH), BF16)
        kib[...] = kw_ref[:, 0:IDX_DIM].astype(BF16)

    per = (SEQ // tq) // key_steps
    for step in range(key_steps):
        @pl.when(qb // per == step)
        def _():
            _dsa_prompt_body((step + 1) * per * tq, qb, q_ref, qi_ref, wq_ref, o_ref, kb, vb, kib, j_ref, tq)


def _dsa_prompt(pa):
    tq = 128
    nq = SEQ // tq
    kcol = 512
    return pl.pallas_call(
        functools.partial(_dsa_prompt_kernel, tq=tq, key_steps=4),
        grid=(BATCH, nq),
        in_specs=[pl.BlockSpec((tq, D), lambda b, i: (b * nq + i, 0)),
                  pl.BlockSpec((tq, 1024), lambda b, i: (b * nq + i, 3)),
                  pl.BlockSpec((tq, kcol), lambda b, i: (b * nq + i, 8)),
                  pl.BlockSpec((SEQ, kcol), lambda b, i: (b, 4)),
                  pl.BlockSpec((SEQ, kcol), lambda b, i: (b, 5)),
                  pl.BlockSpec((SEQ, kcol), lambda b, i: (b, 8))],
        out_specs=pl.BlockSpec((tq, D), lambda b, i: (b * nq + i, 0)),
        out_shape=jax.ShapeDtypeStruct((NP, D), BF16),
        scratch_shapes=[pltpu.VMEM((SEQ, kcol), BF16), pltpu.VMEM((SEQ, 2 * kcol), BF16),
                        pltpu.VMEM((SEQ, IDX_DIM), BF16), pltpu.VMEM((tq, 1), I32)],
        compiler_params=_cp("arbitrary", "arbitrary"),
        name="dsa_prompt",
    )(pa, pa, pa, pa, pa, pa)


def _dsa_score_kernel(pt_ref, qi_ref, wi_ref, kin_ref, cki_hbm, o_ref, kt, sem, *, layer):
    b = pl.program_id(0)

    def page_copy(j):
        off = pl.multiple_of(j * PAGE, PAGE)
        return pltpu.make_async_copy(cki_hbm.at[layer, pt_ref[b, j]], kt.at[:, pl.ds(off, PAGE)], sem)

    def issue(j, c):
        page_copy(j).start()
        return c

    lax.fori_loop(0, N_PAGES, issue, 0)
    lane = lax.broadcasted_iota(I32, (IDX_DIM, PAGE), 1)
    kt[:, PAST:PAST + PAGE] = jnp.where(lane == 0, kin_ref[0], 0.0)

    def wait(j, c):
        page_copy(j).wait()
        return c

    lax.fori_loop(0, N_PAGES, wait, 0)
    s = jnp.dot(qi_ref[0].astype(BF16), kt[...].astype(BF16), preferred_element_type=F32)
    o_ref[0] = jnp.sum(wi_ref[0] * jnp.maximum(s, 0.0), axis=0, keepdims=True)


def _dsa_score(page_table, qi, wi, ki_new, cache_kidx, layer):
    grid_spec = pltpu.PrefetchScalarGridSpec(
        num_scalar_prefetch=1,
        grid=(DEC_BATCH,),
        in_specs=[pl.BlockSpec((1, IDX_HEADS, IDX_DIM), lambda b, pt: (b, 0, 0)),
                  pl.BlockSpec((1, IDX_HEADS, 1), lambda b, pt: (b, 0, 0)),
                  pl.BlockSpec((1, IDX_DIM, 1), lambda b, pt: (b, 0, 0)),
                  pl.BlockSpec(memory_space=pl.ANY)],
        out_specs=pl.BlockSpec((1, 1, SCORE_N), lambda b, pt: (b, 0, 0)),
        scratch_shapes=[pltpu.VMEM((IDX_DIM, SCORE_N), F32), pltpu.SemaphoreType.DMA])
    out = pl.pallas_call(
        functools.partial(_dsa_score_kernel, layer=layer),
        grid_spec=grid_spec,
        out_shape=jax.ShapeDtypeStruct((DEC_BATCH, 1, SCORE_N), F32),
        compiler_params=_cp("arbitrary"),
        name="dsa_sample_score",
    )(page_table, qi, wi, ki_new, cache_kidx)
    return out.reshape(DEC_BATCH, SCORE_N)


def _dsa_select_kernel(s_ref, o_ref, j_ref, slot_ref):
    n_groups = SCORE_N // LANES
    pos = lax.broadcasted_iota(I32, (DEC_BATCH, SCORE_N), 1)
    sel = _select_topk(s_ref[...], pos <= PAST, pos, TOPK, 15, j_ref)
    self_ = jnp.where(sel, 1.0, 0.0).astype(BF16)

    gp = lax.broadcasted_iota(I32, (SCORE_N, 2 * LANES), 0) // LANES
    gg = lax.broadcasted_iota(I32, (SCORE_N, 2 * LANES), 1)
    before = jnp.dot(self_, jnp.where(gp < gg, 1.0, 0.0).astype(BF16), preferred_element_type=F32)
    tri = jnp.where(lax.broadcasted_iota(I32, (LANES, LANES), 0) <= lax.broadcasted_iota(I32, (LANES, LANES), 1),
                    1.0, 0.0).astype(BF16)
    for g in range(n_groups):
        sl = slice(g * LANES, (g + 1) * LANES)
        within = jnp.dot(self_[:, sl], tri, preferred_element_type=F32)
        slot = before[:, g:g + 1] + within - 1.0
        slot_ref[:, sl] = jnp.where(sel[:, sl], slot, -1.0)

    srow = lax.broadcasted_iota(I32, (TOPK, LANES), 0).astype(F32)
    lane = lax.broadcasted_iota(I32, (TOPK, LANES), 1)

    out = jnp.zeros((TOPK, LANES), F32)
    for b in range(DEC_BATCH):
        def per_group(g, acc):
            off = pl.multiple_of(g * LANES, LANES)
            sl = slot_ref[b:b + 1, pl.ds(off, LANES)]
            return acc + jnp.where(sl == srow, (off + lane).astype(F32), 0.0)

        acc = lax.fori_loop(0, n_groups, per_group, jnp.zeros((TOPK, LANES), F32))
        out = jnp.where(lane == b, jnp.sum(acc, axis=1, keepdims=True), out)
    o_ref[...] = out


def _dsa_select(scores):
    out = pl.pallas_call(
        _dsa_select_kernel,
        grid=(1,),
        in_specs=[pl.BlockSpec((DEC_BATCH, SCORE_N), lambda i: (0, 0))],
        out_specs=pl.BlockSpec((TOPK, LANES), lambda i: (0, 0)),
        out_shape=jax.ShapeDtypeStruct((TOPK, LANES), F32),
        scratch_shapes=[pltpu.VMEM((DEC_BATCH, 1), I32), pltpu.VMEM((DEC_BATCH, SCORE_N), F32)],
        compiler_params=_cp("arbitrary"),
        name="dsa_sample_select",
    )(scores)
    return out[:, :DEC_BATCH].T.astype(I32)


def _dsa_attend_kernel(sel_ref, pt_ref, q_ref, kn_hbm, vn_hbm, ck_hbm, cv_hbm, o_ref, kbuf, vbuf, sem, *, layer):
    b = pl.program_id(0)

    def slot_rows(s):
        return pl.ds(pl.multiple_of(s * A_KV, A_KV), A_KV)

    def issue(s, c):
        pos = sel_ref[b, s]

        @pl.when(pos < PAST)
        def _():
            phys = pt_ref[b, pos // PAGE]
            row = pos % PAGE
            pltpu.make_async_copy(ck_hbm.at[layer, phys, row], kbuf.at[slot_rows(s)], sem.at[0]).start()
            pltpu.make_async_copy(cv_hbm.at[layer, phys, row], vbuf.at[slot_rows(s)], sem.at[1]).start()

        @pl.when(pos >= PAST)
        def _():
            pltpu.make_async_copy(kn_hbm.at[b], kbuf.at[slot_rows(s)], sem.at[0]).start()
            pltpu.make_async_copy(vn_hbm.at[b], vbuf.at[slot_rows(s)], sem.at[1]).start()

        return c

    lax.fori_loop(0, TOPK, issue, 0)

    def wait(s, c):
        pltpu.make_async_copy(kn_hbm.at[b], kbuf.at[slot_rows(s)], sem.at[0]).wait()
        pltpu.make_async_copy(vn_hbm.at[b], vbuf.at[slot_rows(s)], sem.at[1]).wait()
        return c

    lax.fori_loop(0, TOPK, wait, 0)

    n = TOPK * A_KV
    s = lax.dot_general(q_ref[0].astype(BF16), kbuf[...].astype(BF16), (((1,), (1,)), ((), ())),
                        preferred_element_type=F32) * (A_DH ** -0.5)
    col = lax.broadcasted_iota(I32, (A_HEADS, n), 1)
    head = lax.broadcasted_iota(I32, (A_HEADS, n), 0)
    s = jnp.where(col % A_KV == head // (A_HEADS // A_KV), s, NEG_INF)
    m = jnp.max(s, axis=1, keepdims=True)
    p = jnp.exp(s - m)
    l = jnp.sum(p, axis=1, keepdims=True)
    o_ref[0] = jnp.dot(p.astype(BF16), vbuf[...].astype(BF16), preferred_element_type=F32) / l


def _dsa_attend(sel, page_table, q, k_new, v_new, cache_k, cache_v, layer):
    grid_spec = pltpu.PrefetchScalarGridSpec(
        num_scalar_prefetch=2,
        grid=(DEC_BATCH,),
        in_specs=[pl.BlockSpec((1, A_HEADS, A_DH), lambda b, s, pt: (b, 0, 0)),
                  pl.BlockSpec(memory_space=pl.ANY), pl.BlockSpec(memory_space=pl.ANY),
                  pl.BlockSpec(memory_space=pl.ANY), pl.BlockSpec(memory_space=pl.ANY)],
        out_specs=pl.BlockSpec((1, A_HEADS, A_DH), lambda b, s, pt: (b, 0, 0)),
        scratch_shapes=[pltpu.VMEM((TOPK * A_KV, A_DH), F32), pltpu.VMEM((TOPK * A_KV, A_DH), F32),
                        pltpu.SemaphoreType.DMA((2,))])
    return pl.pallas_call(
        functools.partial(_dsa_attend_kernel, layer=layer),
        grid_spec=grid_spec,
        out_shape=jax.ShapeDtypeStruct((DEC_BATCH, A_HEADS, A_DH), F32),
        compiler_params=_cp("arbitrary"),
        name="dsa_sample_attend",
    )(sel, page_table, q, k_new, v_new, cache_k, cache_v)


def _sink_softmax_pv(s, sink, v):
    m = jnp.maximum(jnp.max(s, axis=1, keepdims=True), sink)
    p = jnp.exp(s - m)
    den = jnp.sum(p, axis=1, keepdims=True) + jnp.exp(sink - m)
    return jnp.dot(p.astype(BF16), v, preferred_element_type=F32) / den


def _swa_prompt_kernel(q_ref, kp_ref, kc_ref, vp_ref, vc_ref, sink_ref, o_ref):
    qb = pl.program_id(1)
    k = jnp.concatenate([kp_ref[...], kc_ref[...]], axis=0).astype(BF16)
    v = jnp.concatenate([vp_ref[...], vc_ref[...]], axis=0).astype(BF16)
    qrow = lax.broadcasted_iota(I32, (WINDOW, 2 * WINDOW), 0)
    kcol = lax.broadcasted_iota(I32, (WINDOW, 2 * WINDOW), 1)
    first = jnp.where(qb > 0, 0, WINDOW)
    ok = (kcol > qrow) & (kcol <= qrow + WINDOW) & (kcol >= first)
    bias = jnp.where(ok, 0.0, NEG_INF)
    nkeys = 2 * WINDOW
    zeros = jnp.zeros((nkeys, B_DH), BF16)
    lane = lax.broadcasted_iota(I32, (WINDOW, 2 * B_DH), 1)
    g = B_HEADS // B_KV

    def block_diag(x):
        return jnp.concatenate([jnp.concatenate([x, zeros], axis=1), jnp.concatenate([zeros, x], axis=1)], axis=0)

    pairs = [(kv, kv * g + 2 * i) for kv in range(B_KV) for i in range(g // 2)]
    k2 = [block_diag(k[:, kv * B_DH:(kv + 1) * B_DH]) for kv in range(B_KV)]
    ones2 = block_diag(jnp.ones((nkeys, B_DH), BF16))
    v2 = [jnp.concatenate([block_diag(v[:, kv * B_DH:(kv + 1) * B_DH]), ones2], axis=1) for kv in range(B_KV)]
    scores = [lax.dot_general(q_ref[:, h0 * B_DH:(h0 + 2) * B_DH].astype(BF16), k2[kv], (((1,), (1,)), ((), ())),
                              preferred_element_type=F32) for kv, h0 in pairs]
    probs, sink_terms = [], []
    for (kv, h0), s2 in zip(pairs, scores):
        ps, es = [], []
        for e in range(2):
            s = s2[:, e * nkeys:(e + 1) * nkeys] * (B_DH ** -0.5) + bias
            sink = sink_ref[h0 + e:h0 + e + 1, :]
            m = jnp.broadcast_to(jnp.maximum(jnp.max(s, axis=1, keepdims=True), sink), s.shape)
            ps.append(jnp.exp(s - m).astype(BF16))
            es.append(jnp.exp(sink - m[:, 0:2 * B_DH]))
        probs.append(jnp.concatenate(ps, axis=1))
        sink_terms.append(jnp.where(lane < B_DH, es[0], es[1]))
    for (kv, h0), p2, sink_term in zip(pairs, probs, sink_terms):
        ov = jnp.dot(p2, v2[kv], preferred_element_type=F32)
        o2 = ov[:, 0:2 * B_DH] / (ov[:, 2 * B_DH:4 * B_DH] + sink_term)
        o_ref[:, h0 * B_DH:(h0 + 2) * B_DH] = o2.astype(BF16)


def _swa_prompt(pb, sinks):
    nq = SEQ // WINDOW
    kvw = B_KV * B_DH
    kblk, vblk = D // kvw, D // kvw + 1

    def prev(b, i):
        return b * nq + jnp.maximum(i - 1, 0)

    return pl.pallas_call(
        _swa_prompt_kernel,
        grid=(BATCH, nq),
        in_specs=[pl.BlockSpec((WINDOW, D), lambda b, i: (b * nq + i, 0)),
                  pl.BlockSpec((WINDOW, kvw), lambda b, i: (prev(b, i), kblk)),
                  pl.BlockSpec((WINDOW, kvw), lambda b, i: (b * nq + i, kblk)),
                  pl.BlockSpec((WINDOW, kvw), lambda b, i: (prev(b, i), vblk)),
                  pl.BlockSpec((WINDOW, kvw), lambda b, i: (b * nq + i, vblk)),
                  pl.BlockSpec((B_HEADS, 1), lambda b, i: (0, 0))],
        out_specs=pl.BlockSpec((WINDOW, D), lambda b, i: (b * nq + i, 0)),
        out_shape=jax.ShapeDtypeStruct((NP, D), BF16),
        compiler_params=_cp("arbitrary", "arbitrary"),
        name="swa_prompt",
    )(pb, pb, pb, pb, pb, sinks.reshape(B_HEADS, 1))


def _swa_sample_kernel(q_ref, kn_ref, vn_ref, kbuf_ref, vbuf_ref, sink_ref, o_ref, ko_ref, vo_ref):
    row = lax.broadcasted_iota(I32, (WINDOW, B_KV * B_DH), 0)
    k = jnp.where(row == WINDOW - 1, kn_ref[0], pltpu.roll(kbuf_ref[0], WINDOW - 1, 0))
    v = jnp.where(row == WINDOW - 1, vn_ref[0], pltpu.roll(vbuf_ref[0], WINDOW - 1, 0))
    ko_ref[0] = k
    vo_ref[0] = v
    kb, vb = k.astype(BF16), v.astype(BF16)
    g = B_HEADS // B_KV
    for kv in range(B_KV):
        qg = q_ref[0, kv * g:(kv + 1) * g, :].astype(BF16)
        s = lax.dot_general(qg, kb[:, kv * B_DH:(kv + 1) * B_DH], (((1,), (1,)), ((), ())),
                            preferred_element_type=F32) * (B_DH ** -0.5)
        o_ref[0, kv * g:(kv + 1) * g, :] = _sink_softmax_pv(s, sink_ref[kv * g:(kv + 1) * g, :],
                                                            vb[:, kv * B_DH:(kv + 1) * B_DH])


def _swa_sample(q, k_new, v_new, kbuf, vbuf, sinks):
    kvw = B_KV * B_DH
    state = jax.ShapeDtypeStruct((DEC_BATCH, WINDOW, kvw), F32)
    per_b3 = lambda shape: pl.BlockSpec((1,) + shape, lambda b: (b, 0, 0))
    return pl.pallas_call(
        _swa_sample_kernel,
        grid=(DEC_BATCH,),
        in_specs=[per_b3((B_HEADS, B_DH)), per_b3((1, kvw)), per_b3((1, kvw)),
                  per_b3((WINDOW, kvw)), per_b3((WINDOW, kvw)),
                  pl.BlockSpec((B_HEADS, 1), lambda b: (0, 0))],
        out_specs=[per_b3((B_HEADS, B_DH)), per_b3((WINDOW, kvw)), per_b3((WINDOW, kvw))],
        out_shape=[jax.ShapeDtypeStruct((DEC_BATCH, B_HEADS, B_DH), F32), state, state],
        compiler_params=_cp("arbitrary"),
        name="swa_sample",
    )(q, k_new, v_new, kbuf, vbuf, sinks.reshape(B_HEADS, 1))


def _out_proj_kernel(o_ref, w_ref, x_ref, gate_ref, xo_ref):
    y = jnp.dot(o_ref[...], w_ref[0].astype(BF16), preferred_element_type=F32)
    xo_ref[...] = x_ref[...] + gate_ref[0] * y


def _out_proj(x, o, rows, w, layer, name):
    tm, tn = rows.tm, 512
    return pl.pallas_call(
        _out_proj_kernel,
        grid=(rows.n, D // tn),
        in_specs=[pl.BlockSpec((tm, D), lambda i, j: (i, 0)),
                  pl.BlockSpec((1, D, tn), lambda i, j: (layer, 0, j)),
                  pl.BlockSpec((tm, tn), lambda i, j: (i + rows.off, j)),
                  _mod_spec(rows, 2, width=tn, col=True)],
        out_specs=pl.BlockSpec((tm, tn), lambda i, j: (i + rows.off, j)),
        out_shape=jax.ShapeDtypeStruct((RP, D), F32),
        input_output_aliases={2: 0},
        compiler_params=_cp("arbitrary", "arbitrary"),
        name=name,
    )(o, w, x, rows.mod)


ROW_TM = SAMPLE_ROWS
N_PROMPT_BLOCKS = NP // ROW_TM


def _both_mod_specs(chunk):
    def prompt_idx(i, *_):
        return (jnp.minimum(i, N_PROMPT_BLOCKS - 1) * ROW_TM // SEQ, 0, chunk)

    return [pl.BlockSpec((1, 1, D), prompt_idx), pl.BlockSpec((1, ROW_TM, D), lambda i, *_: (0, 0, chunk))]


def _by_row_kind(fn, *pairs):
    i = pl.program_id(0)

    @pl.when(i < N_PROMPT_BLOCKS)
    def _():
        fn(*[p[0] for p, _ in pairs])

    @pl.when(i >= N_PROMPT_BLOCKS)
    def _():
        fn(*[s[0] for _, s in pairs])


def _ffn_pre_kernel(x_ref, g_ref, shp_ref, shs_ref, scp_ref, scs_ref, wr_ref, *out_refs, moe):
    _by_row_kind(functools.partial(_ffn_pre_rows, x_ref, g_ref, wr_ref, out_refs, moe),
                 (shp_ref, shs_ref), (scp_ref, scs_ref))


def _ffn_pre_rows(x_ref, g_ref, wr_ref, out_refs, moe, shift, scale):
    h = _modulated(x_ref[...], g_ref[...], shift, scale)
    out_refs[0][...] = h
    if not moe:
        return
    logits = jnp.dot(h, wr_ref[0], preferred_element_type=F32, precision=lax.Precision.HIGHEST)
    lane = lax.broadcasted_iota(I32, logits.shape, 1).astype(F32)
    l1 = jnp.where(lane < N_EXP, logits, NEG_INF)
    m1 = jnp.max(l1, axis=1, keepdims=True)
    i1 = jnp.min(jnp.where(l1 == m1, lane, float(LANES)), axis=1, keepdims=True)
    l2 = jnp.where(lane == i1, NEG_INF, l1)
    m2 = jnp.max(l2, axis=1, keepdims=True)
    i2 = jnp.min(jnp.where(l2 == m2, lane, float(LANES)), axis=1, keepdims=True)
    e2 = jnp.exp(m2 - m1)
    g1 = 1.0 / (1.0 + e2)
    g2 = e2 / (1.0 + e2)
    out_refs[1][...] = jnp.where(lane == 0, i1, jnp.where(lane == 1, i2, jnp.where(
        lane == 2, g1, jnp.where(lane == 3, g2, 0.0))))


def _ffn_pre(x, g, mod_p, mod_s, wr, layer):
    moe = wr is not None
    row_blk = lambda w: pl.BlockSpec((ROW_TM, w), lambda i: (i, 0))
    shift, scale = _both_mod_specs(3), _both_mod_specs(4)
    in_specs = [row_blk(D), pl.BlockSpec((1, D), lambda i: (0, 0)), *shift, *scale,
                pl.BlockSpec((1, D, LANES), lambda i: (layer if moe else 0, 0, 0))]
    out_specs = [row_blk(D)] + ([row_blk(LANES)] if moe else [])
    out_shape = [jax.ShapeDtypeStruct((RP, D), F32)] + ([jax.ShapeDtypeStruct((RP, LANES), F32)] if moe else [])
    return pl.pallas_call(
        functools.partial(_ffn_pre_kernel, moe=moe), grid=(RP // ROW_TM,), in_specs=in_specs,
        out_specs=out_specs, out_shape=out_shape, compiler_params=_cp("arbitrary"), name="ffn_pre",
    )(x, g.reshape(1, D), mod_p, mod_s, mod_p, mod_s, wr if moe else jnp.zeros((1, D, LANES), F32))


def _moe_plan(route):
    e = route[:N_TOK, :2].astype(I32).reshape(-1)
    oh = (e[:, None] == jnp.arange(N_EXP, dtype=I32)[None, :]).astype(I32)
    csum = jnp.cumsum(oh, axis=0)
    rank = jnp.take_along_axis(csum - oh, e[:, None], axis=1)[:, 0]
    counts = csum[-1]
    ntile = (counts + FFN_TM - 1) // FFN_TM
    tend = jnp.cumsum(ntile)
    tstart = tend - ntile
    pos = tstart[e] * FFN_TM + rank
    src = jnp.zeros((MOE_TILES * FFN_TM,), I32).at[pos].set(jnp.arange(2 * N_TOK, dtype=I32) // 2)
    t = jnp.arange(MOE_TILES, dtype=I32)
    te = jnp.minimum(jnp.sum((t[:, None] >= tend[None, :]).astype(I32), axis=1), N_EXP - 1)
    total = tend[-1]
    nrows = jnp.where(t < total, jnp.clip(counts[te] - (t - tstart[te]) * FFN_TM, 0, FFN_TM), 0)
    te = jnp.where(t < total, te, te[total - 1])
    p = pos.reshape(N_TOK, 2)
    p = jnp.concatenate([p, jnp.broadcast_to(p[:1], (RP - N_TOK, 2))], axis=0)
    return dict(src=src, te=te, nrows=nrows.astype(I32), p1=p[:, 0], p2=p[:, 1])


N_F = FFN // FFN_TF
GATHER_PER_STEP = -(-FFN_TM // N_F)
GATHER_COPIES = GATHER_PER_STEP * N_F


def _ffn_kernel(te_ref, nr_ref, src_ref, h_hbm, wg_ref, wu_ref, wd_ref, o_ref, xf, xb, sem, *, n_tiles):
    t = pl.program_id(0)
    f = pl.program_id(1)
    n_chunks = (nr_ref[t] + FFN_CH - 1) // FFN_CH

    def row_copy(tile, k):
        r = jnp.minimum(k, FFN_TM - 1)
        return pltpu.make_async_copy(h_hbm.at[pl.ds(src_ref[tile * FFN_TM + r], 1)], xf.at[pl.ds(k, 1)], sem)

    def wait_tile():
        def wait(k, c):
            row_copy(0, k).wait()
            return c

        lax.fori_loop(0, GATHER_COPIES, wait, 0)

    @pl.when((t == 0) & (f == 0))
    def _():
        def issue(k, c):
            row_copy(0, k).start()
            return c

        lax.fori_loop(0, GATHER_COPIES, issue, 0)

    @pl.when(f == 0)
    def _():
        wait_tile()
        xb[...] = xf[0:FFN_TM, :].astype(BF16)
        o_ref[...] = jnp.zeros_like(o_ref)

    nxt = jnp.minimum(t + 1, n_tiles - 1)

    def fetch_next():
        for r in range(GATHER_PER_STEP):
            row_copy(nxt, f * GATHER_PER_STEP + r).start()

    for k in range(1, N_CH + 1):
        @pl.when(n_chunks == k)
        def _():
            fetch_next()
            rows = k * FFN_CH
            xc = xb[0:rows, :]
            gate = jnp.dot(xc, wg_ref[0, 0].astype(BF16), preferred_element_type=F32)
            up = jnp.dot(xc, wu_ref[0, 0].astype(BF16), preferred_element_type=F32)
            act = ((gate / (1.0 + jnp.exp(-gate))) * up).astype(BF16)
            o_ref[0:rows, :] += jnp.dot(act, wd_ref[0, 0].astype(BF16), preferred_element_type=F32)

    @pl.when(n_chunks == 0)
    def _():
        fetch_next()

    @pl.when((t == n_tiles - 1) & (f == N_F - 1))
    def _():
        wait_tile()


def _ffn(h, src, te, nrows, wg, wu, wd, layer, n_tiles):
    def fcol(t, f, nr):
        return jnp.where(nr[t] > 0, f, N_F - 1)

    grid_spec = pltpu.PrefetchScalarGridSpec(
        num_scalar_prefetch=3,
        grid=(n_tiles, N_F),
        in_specs=[pl.BlockSpec(memory_space=pl.ANY),
                  pl.BlockSpec((1, 1, D, FFN_TF), lambda t, f, te, nr, src: (layer, te[t], 0, fcol(t, f, nr))),
                  pl.BlockSpec((1, 1, D, FFN_TF), lambda t, f, te, nr, src: (layer, te[t], 0, fcol(t, f, nr))),
                  pl.BlockSpec((1, 1, FFN_TF, D), lambda t, f, te, nr, src: (layer, te[t], fcol(t, f, nr), 0))],
        out_specs=pl.BlockSpec((FFN_TM, D), lambda t, f, te, nr, src: (t, 0)),
        scratch_shapes=[pltpu.VMEM((-(-GATHER_COPIES // 8) * 8, D), F32), pltpu.VMEM((FFN_TM, D), BF16),
                        pltpu.SemaphoreType.DMA])
    return pl.pallas_call(
        functools.partial(_ffn_kernel, n_tiles=n_tiles), grid_spec=grid_spec,
        out_shape=jax.ShapeDtypeStruct((n_tiles * FFN_TM, D), F32),
        compiler_params=_cp("arbitrary", "arbitrary"), name="ffn",
    )(te, nrows, src, h, wg, wu, wd)


def _dense_residual_kernel(x_ref, y_ref, gp_ref, gs_ref, o_ref):
    def rows(gate):
        o_ref[...] = x_ref[...] + gate * y_ref[...]

    _by_row_kind(rows, (gp_ref, gs_ref))


def _dense_residual(x, y, mod_p, mod_s):
    blk = pl.BlockSpec((ROW_TM, D), lambda i: (i, 0))
    return pl.pallas_call(
        _dense_residual_kernel, grid=(RP // ROW_TM,),
        in_specs=[blk, blk, *_both_mod_specs(5)], out_specs=blk,
        out_shape=jax.ShapeDtypeStruct((RP, D), F32), input_output_aliases={0: 0},
        compiler_params=_cp("arbitrary"), name="dense_residual",
    )(x, y, mod_p, mod_s)


def _moe_combine_kernel(p1_ref, p2_ref, x_ref, r_ref, gp_ref, gs_ref, y_hbm, o_ref, ybuf, sem):
    base = pl.program_id(0) * ROW_TM

    def row_copy(r, k, p):
        return pltpu.make_async_copy(y_hbm.at[pl.ds(p, 1)], ybuf.at[k, pl.ds(r, 1)], sem.at[k])

    def issue(r, c):
        row_copy(r, 0, p1_ref[base + r]).start()
        row_copy(r, 1, p2_ref[base + r]).start()
        return c

    lax.fori_loop(0, ROW_TM, issue, 0)

    def wait(r, c):
        row_copy(r, 0, 0).wait()
        row_copy(r, 1, 0).wait()
        return c

    lax.fori_loop(0, ROW_TM, wait, 0)

    def rows(gate):
        mix = r_ref[:, 2:3] * ybuf[0] + r_ref[:, 3:4] * ybuf[1]
        o_ref[...] = x_ref[...] + gate * mix

    _by_row_kind(rows, (gp_ref, gs_ref))


def _moe_combine(x, y, route, plan, mod_p, mod_s):
    blk = lambda w: pl.BlockSpec((ROW_TM, w), lambda i, p1, p2: (i, 0))
    grid_spec = pltpu.PrefetchScalarGridSpec(
        num_scalar_prefetch=2,
        grid=(RP // ROW_TM,),
        in_specs=[blk(D), blk(LANES), *_both_mod_specs(5), pl.BlockSpec(memory_space=pl.ANY)],
        out_specs=blk(D),
        scratch_shapes=[pltpu.VMEM((2, ROW_TM, D), F32), pltpu.SemaphoreType.DMA((2,))])
    return pl.pallas_call(
        _moe_combine_kernel, grid_spec=grid_spec,
        out_shape=jax.ShapeDtypeStruct((RP, D), F32), input_output_aliases={2: 0},
        compiler_params=_cp("arbitrary"), name="moe_combine",
    )(plan["p1"], plan["p2"], x, route, mod_p, mod_s, y)


def _final_norm_kernel(x_ref, g_ref, op_ref, os_ref):
    x = x_ref[...]
    ms = jnp.mean(x * x, axis=-1, keepdims=True)
    y = x * lax.rsqrt(ms + EPS) * g_ref[...]
    i = pl.program_id(0)

    @pl.when(i < N_PROMPT_BLOCKS)
    def _():
        op_ref[...] = y

    @pl.when(i >= N_PROMPT_BLOCKS)
    def _():
        os_ref[...] = y


def _final_norm(x, g):
    return pl.pallas_call(
        _final_norm_kernel, grid=(RP // ROW_TM,),
        in_specs=[pl.BlockSpec((ROW_TM, D), lambda i: (i, 0)), pl.BlockSpec((1, D), lambda i: (0, 0))],
        out_specs=[pl.BlockSpec((ROW_TM, D), lambda i: (jnp.minimum(i, N_PROMPT_BLOCKS - 1), 0)),
                   pl.BlockSpec((ROW_TM, D), lambda i: (0, 0))],
        out_shape=[jax.ShapeDtypeStruct((NP, D), F32), jax.ShapeDtypeStruct((ROW_TM, D), F32)],
        compiler_params=_cp("arbitrary"), name="final_norm",
    )(x, g.reshape(1, D))


A_TILES = (T_ROPE128,) * 5 + (T_PLAIN,) + (T_ROPE64,) * 2 + (T_KIWI,)
B_TILES = (T_ROPE64,) * 9 + (T_PLAIN,)


def kernel(x_prompt, x_sample, cache_k_a, cache_v_a, cache_kidx_a, state_k_b, state_v_b, page_table,
           c_prompt, c_sample, w_ada, b_ada, g_attn, g_ffn, w_in_a, w_out_a, w_in_b, w_out_b, sinks_b,
           w_gate_d, w_up_d, w_down_d, w_router, w_gate_e, w_up_e, w_down_e, g_final):
    x = jnp.concatenate([x_prompt.reshape(NP, D), x_sample.reshape(DEC_BATCH, D),
                         jnp.zeros((SAMPLE_ROWS - DEC_BATCH, D), F32)], axis=0)
    c_all = jnp.concatenate([c_prompt, c_sample, jnp.zeros((4, D), F32)], axis=0)
    mods = _adaln(c_all, w_ada, b_ada)
    mod_p = mods[:, :BATCH].reshape(DEPTH, BATCH, 1, 6 * D)
    mod_s = jnp.pad(mods[:, BATCH:BATCH + DEC_BATCH], ((0, 0), (0, SAMPLE_ROWS - DEC_BATCH), (0, 0)))
    mod_s = mod_s.reshape(DEPTH, 1, SAMPLE_ROWS, 6 * D)

    tab_p = _rope_tables(np.arange(SEQ))
    tab_s = _rope_tables(np.full((SAMPLE_ROWS,), PAST))
    w_in_a_pad = jnp.pad(w_in_a, ((0, 0), (0, 0), (0, A_PROJ_PAD - A_PROJ)))
    w_router_pad = jnp.pad(w_router, ((0, 0), (0, 0), (0, LANES - N_EXP)))
    kvw = B_KV * B_DH
    cache_kidx_t = jnp.swapaxes(cache_kidx_a, 2, 3)
    dense_src = jnp.asarray(np.minimum(np.arange(DENSE_TILES * FFN_TM), RP - 1).astype(np.int32))
    dense_te = jnp.zeros((DENSE_TILES,), I32)
    dense_rows = jnp.asarray(np.array([min(FFN_TM, RP - t * FFN_TM) for t in range(DENSE_TILES)], np.int32))

    ka_p, va_p, ia_p, ka_s, va_s, ia_s, kb_p, vb_p, kb_s, vb_s = ([] for _ in range(10))
    for i in range(DEPTH):
        j = i // 2
        rp = lambda tm: _Rows(tm, NP // tm, 0, mod_p[i], False)
        rs = _Rows(SAMPLE_ROWS, 1, NP // SAMPLE_ROWS, mod_s[i], True)
        if i % 2 == 0:
            pa = _proj(x, rp(1024), g_attn[i], 0, w_in_a_pad, j, tab_p, A_TILES, 512, "proj_a_prompt")
            sa = _proj(x, rs, g_attn[i], 0, w_in_a_pad, j, tab_s, A_TILES, 512, "proj_a_sample")
            ka_p.append(pa[:, 2048:2560].reshape(BATCH, SEQ, A_KV, A_DH))
            va_p.append(pa[:, 2560:3072].reshape(BATCH, SEQ, A_KV, A_DH))
            ia_p.append(pa[:, 4096:4160].reshape(BATCH, SEQ, IDX_DIM))
            sa = sa[:DEC_BATCH]
            k_new = sa[:, 2048:2560].reshape(DEC_BATCH, A_KV, A_DH)
            v_new = sa[:, 2560:3072].reshape(DEC_BATCH, A_KV, A_DH)
            ki_new = sa[:, 4096:4160].reshape(DEC_BATCH, 1, IDX_DIM)
            ka_s.append(k_new.reshape(DEC_BATCH, 1, A_KV, A_DH))
            va_s.append(v_new.reshape(DEC_BATCH, 1, A_KV, A_DH))
            ia_s.append(ki_new)
            o_p = _dsa_prompt(pa)
            scores = _dsa_score(page_table, sa[:, 3072:4096].reshape(DEC_BATCH, IDX_HEADS, IDX_DIM),
                                sa[:, 4160:4176].reshape(DEC_BATCH, IDX_HEADS, 1),
                                ki_new.reshape(DEC_BATCH, IDX_DIM, 1), cache_kidx_t, j)
            sel = _dsa_select(scores)
            o_s = _dsa_attend(sel, page_table, sa[:, :2048].reshape(DEC_BATCH, A_HEADS, A_DH), k_new, v_new,
                              cache_k_a, cache_v_a, j)
            w_out = w_out_a
        else:
            pb = _proj(x, rp(1024), g_attn[i], 0, w_in_b, j, tab_p, B_TILES, 256, "proj_b_prompt")
            sb = _proj(x, rs, g_attn[i], 0, w_in_b, j, tab_s, B_TILES, 256, "proj_b_sample")
            kb_p.append(pb[:, D:D + kvw].reshape(BATCH, SEQ, B_KV, B_DH)[:, SEQ - WINDOW:])
            vb_p.append(pb[:, D + kvw:].reshape(BATCH, SEQ, B_KV, B_DH)[:, SEQ - WINDOW:])
            sb = sb[:DEC_BATCH]
            o_p = _swa_prompt(pb, sinks_b[j])
            o_s, k_st, v_st = _swa_sample(sb[:, :D].reshape(DEC_BATCH, B_HEADS, B_DH),
                                          sb[:, D:D + kvw].reshape(DEC_BATCH, 1, kvw),
                                          sb[:, D + kvw:].reshape(DEC_BATCH, 1, kvw),
                                          state_k_b[j].reshape(DEC_BATCH, WINDOW, kvw),
                                          state_v_b[j].reshape(DEC_BATCH, WINDOW, kvw), sinks_b[j])
            kb_s.append(k_st.reshape(DEC_BATCH, WINDOW, B_KV, B_DH))
            vb_s.append(v_st.reshape(DEC_BATCH, WINDOW, B_KV, B_DH))
            w_out = w_out_b
        o_s = jnp.pad(o_s.reshape(DEC_BATCH, D), ((0, SAMPLE_ROWS - DEC_BATCH), (0, 0))).astype(BF16)
        x = _out_proj(x, o_p, rp(2048), w_out, j, "out_proj_prompt")
        x = _out_proj(x, o_s, rs, w_out, j, "out_proj_sample")

        if i % 2 == 0:
            (h,) = _ffn_pre(x, g_ffn[i], mod_p[i], mod_s[i], None, j)
            y = _ffn(h, dense_src, dense_te, dense_rows, w_gate_d[:, None], w_up_d[:, None], w_down_d[:, None],
                     j, DENSE_TILES)
            x = _dense_residual(x, y, mod_p[i], mod_s[i])
        else:
            h, route = _ffn_pre(x, g_ffn[i], mod_p[i], mod_s[i], w_router_pad, j)
            plan = _moe_plan(route)
            y = _ffn(h, plan["src"], plan["te"], plan["nrows"], w_gate_e, w_up_e, w_down_e, j, MOE_TILES)
            x = _moe_combine(x, y, route, plan, mod_p[i], mod_s[i])

    y_p, y_s = _final_norm(x, g_final)
    return (y_p.reshape(BATCH, SEQ, D), y_s[:DEC_BATCH].reshape(DEC_BATCH, 1, D),
            jnp.stack(ka_p), jnp.stack(va_p), jnp.stack(ia_p),
            jnp.stack(ka_s), jnp.stack(va_s), jnp.stack(ia_s),
            jnp.stack(kb_p), jnp.stack(vb_p), jnp.stack(kb_s), jnp.stack(vb_s))
```
